```python
import jax
import jax.numpy as jnp
from jax import lax
import numpy as np

D_MODEL = 1024
BATCH = 2
SEQ = 16384
DEPTH = 1
DEC_BATCH = 8
DEC_SEQ = 8192
PAST_LEN = 128

HEAD_DIM = 64
RWKV_HEADS = 8
RWKV_WIDTH = RWKV_HEADS * HEAD_DIM
RWKV_DECAY_RANK = 64
RWKV_ICLR_RANK = 64
RWKV_GATE_RANK = 128
RWKV_GN_EPS = 64e-5
RET_HEADS = 8
RET_WIDTH = RET_HEADS * HEAD_DIM
RET_CHUNK = 128
RET_GN_EPS = 1e-6
ROPE_BASE = 10000.0
PEER_HEADS = 8
PEER_NKEYS = 128
PEER_EXPERTS = PEER_NKEYS * PEER_NKEYS
PEER_QDIM = 256
PEER_HALF = PEER_QDIM // 2
PEER_TOPK = 16
PEER_BLOCK = 128
NORM_EPS = 1e-6

RWKV_SPLITS = (RWKV_WIDTH, RWKV_WIDTH, RWKV_WIDTH, RWKV_DECAY_RANK, RWKV_DECAY_RANK, RWKV_ICLR_RANK, RWKV_ICLR_RANK, RWKV_GATE_RANK)
REST_SPLITS = (RET_WIDTH, RET_WIDTH, RET_WIDTH, RET_WIDTH, D_MODEL, D_MODEL)
RWKV_COLS = sum(RWKV_SPLITS)
IN_COLS = RWKV_COLS + sum(REST_SPLITS)

kernel_name = 'hybrid_rwkv7_retention_peer_encoder'


def _rmsnorm(x, g):
    xf = x.astype(jnp.float32)
    y = xf * lax.rsqrt(jnp.mean(xf * xf, axis=-1, keepdims=True) + NORM_EPS)
    return (y * g.astype(jnp.float32)).astype(x.dtype)


def _split_cols(p, sizes):
    outs = []
    start = 0
    for s in sizes:
        outs.append(p[..., start:start + s])
        start += s
    return outs


def _to_heads(t, n_heads):
    b, s, w = t.shape
    return t.astype(jnp.float32).reshape(b, s, n_heads, w // n_heads)


def _head_norm(o, eps):
    mu = jnp.mean(o, axis=-1, keepdims=True)
    var = jnp.mean(jnp.square(o - mu), axis=-1, keepdims=True)
    y = (o - mu) * lax.rsqrt(var + eps)
    b, s, h, n = o.shape
    return y.reshape(b, s, h * n)


def _centred_shift_mix(p, mu):
    prev = jnp.pad(p[:, :-1], ((0, 0), (1, 0), (0, 0)))
    nxt = jnp.pad(p[:, 1:], ((0, 0), (0, 1), (0, 0)))
    return p + mu[0] * (prev - p) + mu[1] * (nxt - p)


def _rwkv7_scan(r, w, k, v, kk, a, reverse):
    b, s, h, n = r.shape

    def step(state, inp):
        r_t, w_t, k_t, v_t, kk_t, a_t = inp
        sa = jnp.einsum('bhvk,bhk->bhv', state, -kk_t)
        state = (state * w_t[:, :, None, :]
                 + sa[..., None] * (kk_t * a_t)[:, :, None, :]
                 + v_t[..., None] * k_t[:, :, None, :])
        return state, jnp.einsum('bhvk,bhk->bhv', state, r_t)

    xs = tuple(jnp.moveaxis(t, 1, 0) for t in (r, w, k, v, kk, a))
    state0 = jnp.zeros((b, h, n, n), jnp.float32)
    _, out = lax.scan(step, state0, xs, reverse=reverse)
    return jnp.moveaxis(out, 0, 1)


def _rwkv7_branch(pr, pk, pv, pwf, pwb, paf, pab, pg, w0, w2, a0, a2, g2, k_k, k_a, r_k, ln_g, ln_b):
    f32 = jnp.float32
    H, N = RWKV_HEADS, HEAD_DIM
    b, s, _ = pr.shape
    r = _to_heads(pr, H)
    k = _to_heads(pk, H)
    v = _to_heads(pv, H)
    kk = k * k_k.astype(f32).reshape(H, N)
    kk = kk / jnp.maximum(jnp.sqrt(jnp.sum(kk * kk, axis=-1, keepdims=True)), 1e-12)
    k_a_h = k_a.astype(f32).reshape(H, N)
    outs = []
    keys = []
    for d, (wl, al) in enumerate(((pwf, paf), (pwb, pab))):
        w_log = -jax.nn.softplus(-(w0[d].astype(f32) + jnp.tanh(wl.astype(f32)) @ w2[d].astype(f32))) - 0.5
        decay = _to_heads(jnp.exp(-jnp.exp(w_log)), H)
        a = _to_heads(jax.nn.sigmoid(a0[d].astype(f32) + al.astype(f32) @ a2[d].astype(f32)), H)
        k_d = k * (1.0 + (a - 1.0) * k_a_h)
        outs.append(_rwkv7_scan(r, decay, k_d, v, kk, a, reverse=(d == 1)))
        keys.append(k_d)
    o = _head_norm(outs[0] + outs[1], RWKV_GN_EPS) * ln_g.astype(f32) + ln_b.astype(f32)
    bonus = jnp.sum(r * (0.5 * (keys[0] + keys[1])) * r_k.astype(f32), axis=-1, keepdims=True) * v
    g = jax.nn.sigmoid(pg.astype(f32)) @ g2.astype(f32)
    return (o + bonus.reshape(b, s, H * N)) * g


def _rotary(t, pos):
    half = HEAD_DIM // 2
    inv = ROPE_BASE ** (-jnp.arange(half, dtype=jnp.float32) / half)
    ang = pos[:, None] * inv[None, :]
    c = jnp.cos(ang)[None, :, None, :]
    s = jnp.sin(ang)[None, :, None, :]
    t1, t2 = t[..., :half], t[..., half:]
    return jnp.concatenate([t1 * c - t2 * s, t1 * s + t2 * c], axis=-1)


def _retention_chunked(q, k, v, log_gamma, strict):
    b, h, s, dk = q.shape
    dv = v.shape[-1]
    C = RET_CHUNK
    nc = s // C
    qc = q.reshape(b, h, nc, C, dk)
    kc = k.reshape(b, h, nc, C, dk)
    vc = v.reshape(b, h, nc, C, dv)
    idx = jnp.arange(C, dtype=jnp.float32)
    diff = idx[:, None] - idx[None, :]
    mask = diff > 0 if strict else diff >= 0
    decay_in = jnp.where(mask[None], jnp.exp(jnp.where(mask, diff, 0.0)[None] * log_gamma[:, None, None]), 0.0)
    scores = jnp.einsum('bhncd,bhnjd->bhncj', qc, kc) * decay_in[:, None]
    inner = jnp.einsum('bhncj,bhnje->bhnce', scores, vc)
    zeta = jnp.exp((C - 1.0 - idx)[None, :] * log_gamma[:, None])
    xi = jnp.exp((idx + 1.0)[None, :] * log_gamma[:, None])
    gamma_c = jnp.exp(C * log_gamma)
    kv_chunk = jnp.einsum('bhncd,bhnce->bhnde', kc * zeta[:, None, :, None], vc)

    def step(R, kv):
        return gamma_c[None, :, None, None] * R + kv, R

    _, r_prev = lax.scan(step, jnp.zeros((b, h, dk, dv), jnp.float32), jnp.moveaxis(kv_chunk, 2, 0))
    r_prev = jnp.moveaxis(r_prev, 0, 2)
    cross = jnp.einsum('bhncd,bhnde->bhnce', qc, r_prev) * xi[:, None, :, None]
    return (inner + cross).reshape(b, h, s, dv)


def _retention_branch(pq, pk, pv, pg, ln_g):
    f32 = jnp.float32
    b, s, _ = pq.shape
    pos = jnp.arange(s, dtype=f32)
    q = jnp.transpose(_rotary(_to_heads(pq, RET_HEADS), pos) * (HEAD_DIM ** -0.5), (0, 2, 1, 3))
    k = jnp.transpose(_rotary(_to_heads(pk, RET_HEADS), pos), (0, 2, 1, 3))
    v = jnp.transpose(_to_heads(pv, RET_HEADS), (0, 2, 1, 3))
    log_gamma = jnp.log1p(-jnp.exp2(-5.0 - jnp.arange(RET_HEADS, dtype=f32)))
    fwd = _retention_chunked(q, k, v, log_gamma, strict=False)
    bwd = jnp.flip(_retention_chunked(jnp.flip(q, 2), jnp.flip(k, 2), jnp.flip(v, 2), log_gamma, strict=True), 2)
    o = jnp.transpose(fwd + bwd, (0, 2, 1, 3))
    o = _head_norm(o, RET_GN_EPS) * ln_g.astype(f32)
    return o * jax.nn.silu(pg.astype(f32))


def _peer(h, wq, subkeys, u, v):
    b, s, d = h.shape
    hb_all = h.reshape((b * s) // PEER_BLOCK, PEER_BLOCK, d)

    def block(hb):
        P = hb.shape[0]
        q = (hb @ wq).reshape(P, PEER_HEADS, 2, PEER_HALF)
        sc = jnp.einsum('phcd,hcnd->phcn', q, subkeys).astype(jnp.float32)
        s1, i1 = lax.top_k(sc[:, :, 0], PEER_TOPK)
        s2, i2 = lax.top_k(sc[:, :, 1], PEER_TOPK)
        cand_s = (s1[..., :, None] + s2[..., None, :]).reshape(P, PEER_HEADS, PEER_TOPK * PEER_TOPK)
        cand_i = (i1[..., :, None] * PEER_NKEYS + i2[..., None, :]).reshape(P, PEER_HEADS, PEER_TOPK * PEER_TOPK)
        top_s, pos = lax.top_k(cand_s, PEER_TOPK)
        eidx = jnp.take_along_axis(cand_i, pos, axis=-1)
        gate = jax.nn.softmax(top_s, axis=-1)
        ue = jnp.take(u, eidx, axis=0)
        act = jax.nn.gelu(jnp.einsum('phkd,pd->phk', ue, hb).astype(jnp.float32), approximate=False)
        ve = jnp.take(v, eidx, axis=0)
        return jnp.einsum('phk,phkd->pd', (gate * act).astype(hb.dtype), ve)

    return lax.map(block, hb_all).reshape(b, s, d)


def _layer(x, norm1_g, w_in, rwkv_mu, rwkv_w0, rwkv_w2, rwkv_a0, rwkv_a2, rwkv_g2, rwkv_k_k, rwkv_k_a,
           rwkv_r_k, rwkv_ln_g, rwkv_ln_b, ret_ln_g, proj_a, proj_b, w_out, norm2_g, peer_wq,
           peer_subkeys, peer_u, peer_v):
    h = _rmsnorm(x, norm1_g)
    p = h @ w_in
    p_rwkv = _centred_shift_mix(p[..., :RWKV_COLS], rwkv_mu)
    pr, pk, pv, pwf, pwb, paf, pab, pg = _split_cols(p_rwkv, RWKV_SPLITS)
    qq, qk, qv, qg, gate_a, gate_b = _split_cols(p[..., RWKV_COLS:], REST_SPLITS)
    o_a = _rwkv7_branch(pr, pk, pv, pwf, pwb, paf, pab, pg, rwkv_w0, rwkv_w2, rwkv_a0, rwkv_a2,
                        rwkv_g2, rwkv_k_k, rwkv_k_a, rwkv_r_k, rwkv_ln_g, rwkv_ln_b)
    o_b = _retention_branch(qq, qk, qv, qg, ret_ln_g)
    y_a = o_a.astype(x.dtype) @ proj_a
    y_b = o_b.astype(x.dtype) @ proj_b
    mixed = jax.nn.sigmoid(gate_a) * y_a + jax.nn.sigmoid(gate_b) * y_b
    x = x + mixed @ w_out
    x = x + _peer(_rmsnorm(x, norm2_g), peer_wq, peer_subkeys, peer_u, peer_v)
    return x


def _trunk(x, layer_w, normf_g):
    for l in range(DEPTH):
        x = _layer(x, *[w[l] for w in layer_w])
    return _rmsnorm(x, normf_g)


def setup_inputs(seed: int = 0) -> dict:
    key = jax.random.key(seed)
    ks = jax.random.split(key, 26)
    f32 = jnp.float32
    nrm = lambda k, shape, scale: jax.random.normal(k, shape, f32) * scale
    L = DEPTH
    return {
        'x_prompt': nrm(ks[0], (BATCH, SEQ, D_MODEL), 1.0),
        'x_sample': nrm(ks[1], (DEC_BATCH, DEC_SEQ, D_MODEL), 1.0),
        'norm1_g': 1.0 + nrm(ks[2], (L, D_MODEL), 0.02),
        'w_in': nrm(ks[3], (L, D_MODEL, IN_COLS), D_MODEL ** -0.5),
        'rwkv_mu': jax.random.uniform(ks[4], (L, 2, RWKV_COLS), f32, 0.0, 0.5),
        'rwkv_w0': jax.random.uniform(ks[5], (L, 2, RWKV_WIDTH), f32, -6.0, 1.0),
        'rwkv_w2': nrm(ks[6], (L, 2, RWKV_DECAY_RANK, RWKV_WIDTH), 0.5 * RWKV_DECAY_RANK ** -0.5),
        'rwkv_a0': nrm(ks[7], (L, 2, RWKV_WIDTH), 0.1),
        'rwkv_a2': nrm(ks[8], (L, 2, RWKV_ICLR_RANK, RWKV_WIDTH), 0.5 * RWKV_ICLR_RANK ** -0.5),
        'rwkv_g2': nrm(ks[9], (L, RWKV_GATE_RANK, RWKV_WIDTH), RWKV_GATE_RANK ** -0.5),
        'rwkv_k_k': 0.85 + nrm(ks[10], (L, RWKV_WIDTH), 0.02),
        'rwkv_k_a': 1.0 + nrm(ks[11], (L, RWKV_WIDTH), 0.02),
        'rwkv_r_k': nrm(ks[12], (L, RWKV_HEADS, HEAD_DIM), 0.1),
        'rwkv_ln_g': 1.0 + nrm(ks[13], (L, RWKV_WIDTH), 0.02),
        'rwkv_ln_b': nrm(ks[14], (L, RWKV_WIDTH), 0.02),
        'ret_ln_g': 1.0 + nrm(ks[15], (L, RET_WIDTH), 0.02),
        'proj_a': nrm(ks[16], (L, RWKV_WIDTH, D_MODEL), RWKV_WIDTH ** -0.5),
        'proj_b': nrm(ks[17], (L, RET_WIDTH, D_MODEL), RET_WIDTH ** -0.5),
        'w_out': nrm(ks[18], (L, D_MODEL, D_MODEL), D_MODEL ** -0.5),
        'norm2_g': 1.0 + nrm(ks[19], (L, D_MODEL), 0.02),
        'peer_wq': nrm(ks[20], (L, D_MODEL, PEER_HEADS * PEER_QDIM), D_MODEL ** -0.5),
        'peer_subkeys': nrm(ks[21], (L, PEER_HEADS, 2, PEER_NKEYS, PEER_HALF), PEER_HALF ** -0.5),
        'peer_u': nrm(ks[22], (L, PEER_EXPERTS, D_MODEL), D_MODEL ** -0.5),
        'peer_v': nrm(ks[23], (L, PEER_EXPERTS, D_MODEL), D_MODEL ** -0.5),
        'normf_g': 1.0 + nrm(ks[24], (D_MODEL,), 0.02),
    }


def reference(x_prompt, x_sample, norm1_g, w_in, rwkv_mu, rwkv_w0, rwkv_w2, rwkv_a0, rwkv_a2, rwkv_g2,
              rwkv_k_k, rwkv_k_a, rwkv_r_k, rwkv_ln_g, rwkv_ln_b, ret_ln_g, proj_a, proj_b, w_out,
              norm2_g, peer_wq, peer_subkeys, peer_u, peer_v, normf_g):
    layer_w = (norm1_g, w_in, rwkv_mu, rwkv_w0, rwkv_w2, rwkv_a0, rwkv_a2, rwkv_g2, rwkv_k_k, rwkv_k_a,
               rwkv_r_k, rwkv_ln_g, rwkv_ln_b, ret_ln_g, proj_a, proj_b, w_out, norm2_g, peer_wq,
               peer_subkeys, peer_u, peer_v)
    y_prompt = _trunk(x_prompt, layer_w, normf_g)
    y_sample = _trunk(x_sample, layer_w, normf_g)
    return (y_prompt, y_sample)
```

```python
import functools

import jax
import jax.numpy as jnp
from jax import lax
from jax.experimental import pallas as pl
from jax.experimental.pallas import tpu as pltpu

F32 = jnp.float32
BF16 = jnp.bfloat16

D_MODEL = 1024
HEAD_DIM = 64
N_HEADS = 8
WIDTH = N_HEADS * HEAD_DIM
RWKV_COLS = 1920
RET_COLS = 2048
GATE_COLS = 2048
RWKV_GN_EPS = 64e-5
RET_GN_EPS = 1e-6
NORM_EPS = 1e-6
ROPE_BASE = 10000.0
PEER_HEADS = 8
PEER_NKEYS = 128
PEER_TOPK = 16
PEER_EXPERTS = PEER_NKEYS * PEER_NKEYS

LANES = 128
CHUNK = 64
SEQ_TILE = 256
TOK_TILE = 256
PEER_TOK_TILE = 512
PEER_EXP_TILE = 1024
TOPK_TILE = 512
VMEM_LIMIT = 56 * 1024 * 1024
NEG_BIG = -1e30

_NT = (((1,), (1,)), ((), ()))


def _cparams(sem):
    return pltpu.CompilerParams(dimension_semantics=sem, vmem_limit_bytes=VMEM_LIMIT)


def _dot(a, b):
    return jnp.dot(a, b, preferred_element_type=F32)


def _split2(x):
    hi = x.astype(BF16)
    lo = (x - hi.astype(F32)).astype(BF16)
    return hi, lo


def _mm3(a, b):
    ah, al = _split2(a)
    bh, bl = _split2(b)
    return _dot(ah, bh) + (_dot(ah, bl) + _dot(al, bh))


def _mm3_nt(a, b):
    ah, al = _split2(a)
    bh, bl = _split2(b)
    f = lambda x, y: lax.dot_general(x, y, _NT, preferred_element_type=F32)
    return f(ah, bh) + (f(ah, bl) + f(al, bh))


def _ones_mm(m_bf16, x):
    x0 = x.astype(BF16)
    r1 = x - x0.astype(F32)
    x1 = r1.astype(BF16)
    x2 = (r1 - x1.astype(F32)).astype(BF16)
    return _dot(m_bf16, x0) + (_dot(m_bf16, x1) + _dot(m_bf16, x2))


def _segsum(x, bd):
    hi, lo = _split2(x)
    return _dot(hi, bd) + _dot(lo, bd)


def _rmsnorm(x, g):
    return x * lax.rsqrt(jnp.mean(x * x, axis=-1, keepdims=True) + NORM_EPS) * g


def _inproj_kernel(x_ref, g_ref, w_ref, o1_ref, o2_ref, o3_ref):
    hb = _rmsnorm(x_ref[...], g_ref[...]).astype(BF16)
    o1_ref[...] = _dot(hb, w_ref[:, 0:RWKV_COLS])
    o2_ref[...] = _dot(hb, w_ref[:, RWKV_COLS:RWKV_COLS + RET_COLS])
    o3_ref[...] = _dot(hb, w_ref[:, RWKV_COLS + RET_COLS:])


def _inproj(x2, g, w_bf16):
    T = x2.shape[0]
    tm = TOK_TILE
    ncol = w_bf16.shape[1]
    return pl.pallas_call(
        _inproj_kernel,
        name="inproj",
        grid=(T // tm,),
        in_specs=[
            pl.BlockSpec((tm, D_MODEL), lambda i: (i, 0)),
            pl.BlockSpec((1, D_MODEL), lambda i: (0, 0)),
            pl.BlockSpec((D_MODEL, ncol), lambda i: (0, 0)),
        ],
        out_specs=[
            pl.BlockSpec((tm, RWKV_COLS), lambda i: (i, 0)),
            pl.BlockSpec((tm, RET_COLS), lambda i: (i, 0)),
            pl.BlockSpec((tm, GATE_COLS), lambda i: (i, 0)),
        ],
        out_shape=[
            jax.ShapeDtypeStruct((T, RWKV_COLS), F32),
            jax.ShapeDtypeStruct((T, RET_COLS), F32),
            jax.ShapeDtypeStruct((T, GATE_COLS), F32),
        ],
        compiler_params=_cparams(("parallel",)),
    )(x2, g, w_bf16)


def _softplus(y):
    return jnp.maximum(y, 0.0) + jnp.log1p(jnp.exp(-jnp.abs(y)))


def _rwkv_prep_kernel(p_ref, pprev_ref, pnext_ref, mu_ref, w0_ref, w2_ref, a0_ref, a2_ref, g2_ref,
                      kk_ref, ka_ref, rk_ref, bd_ref,
                      r_ref, v_ref, kap_ref, lw_ref, kd_ref, b_ref, bonus_ref, g_ref):
    i = pl.program_id(1)
    n = pl.num_programs(1)
    p = p_ref[0]
    tm = p.shape[0]
    prev_row = jnp.where(i > 0, pprev_ref[0, 7:8, :], 0.0)
    next_row = jnp.where(i < n - 1, pnext_ref[0, 0:1, :], 0.0)
    row = lax.broadcasted_iota(jnp.int32, p.shape, 0)
    prev = jnp.where(row == 0, prev_row, pltpu.roll(p, 1, 0))
    nxt = jnp.where(row == tm - 1, next_row, pltpu.roll(p, tm - 1, 0))
    ps = p + mu_ref[0:1, :] * (prev - p) + mu_ref[1:2, :] * (nxt - p)

    bd = bd_ref[...]
    r = ps[:, 0:WIDTH]
    k = ps[:, WIDTH:2 * WIDTH]
    v = ps[:, 2 * WIDTH:3 * WIDTH]
    kk = k * kk_ref[...]
    kappa = kk / jnp.maximum(jnp.sqrt(_segsum(kk * kk, bd)), 1e-12)
    r_ref[0] = r
    v_ref[0] = v
    kap_ref[0] = kappa
    ksum = jnp.zeros_like(k)
    for d in range(2):
        wl = ps[:, 1536 + 64 * d:1600 + 64 * d]
        al = ps[:, 1664 + 64 * d:1728 + 64 * d]
        z = w0_ref[d:d + 1, :] + _dot(jnp.tanh(wl).astype(BF16), w2_ref[d])
        w_log = -_softplus(-z) - 0.5
        lw_ref[d, 0] = -jnp.exp(w_log)
        a = jax.nn.sigmoid(a0_ref[d:d + 1, :] + _dot(al.astype(BF16), a2_ref[d]))
        kd = k * (1.0 + (a - 1.0) * ka_ref[...])
        kd_ref[d, 0] = kd
        b_ref[d, 0] = kappa * a
        ksum = ksum + kd
    bonus_ref[0] = _segsum(r * (0.5 * ksum) * rk_ref[...], bd) * v
    g_ref[0] = _dot(jax.nn.sigmoid(ps[:, 1792:1920]).astype(BF16), g2_ref[...])


def _rwkv_prep(p_rwkv, mu, w0, w2, a0, a2, g2, k_k, k_a, r_k, bd):
    B, S, _ = p_rwkv.shape
    tm = SEQ_TILE
    n = S // tm
    nb8 = S // 8
    full = lambda shape: pl.BlockSpec(shape, lambda b, i: (0,) * len(shape))
    tok = pl.BlockSpec((1, tm, WIDTH), lambda b, i: (b, i, 0))
    tok2 = pl.BlockSpec((2, 1, tm, WIDTH), lambda b, i: (0, b, i, 0))
    s1 = jax.ShapeDtypeStruct((B, S, WIDTH), F32)
    s2 = jax.ShapeDtypeStruct((2, B, S, WIDTH), F32)
    return pl.pallas_call(
        _rwkv_prep_kernel,
        name="rwkv_prep",
        grid=(B, n),
        in_specs=[
            pl.BlockSpec((1, tm, RWKV_COLS), lambda b, i: (b, i, 0)),
            pl.BlockSpec((1, 8, RWKV_COLS), lambda b, i: (b, jnp.maximum(i * (tm // 8) - 1, 0), 0)),
            pl.BlockSpec((1, 8, RWKV_COLS), lambda b, i: (b, jnp.minimum((i + 1) * (tm // 8), nb8 - 1), 0)),
            full((2, RWKV_COLS)), full((2, WIDTH)), full((2, 64, WIDTH)), full((2, WIDTH)),
            full((2, 64, WIDTH)), full((128, WIDTH)), full((1, WIDTH)), full((1, WIDTH)), full((1, WIDTH)),
            full((WIDTH, WIDTH)),
        ],
        out_specs=[tok, tok, tok, tok2, tok2, tok2, tok, tok],
        out_shape=[s1, s1, s1, s2, s2, s2, s1, s1],
        compiler_params=_cparams(("parallel", "parallel")),
    )(p_rwkv, p_rwkv, p_rwkv, mu, w0, w2, a0, a2, g2, k_k, k_a, r_k, bd)


def _rwkv_scan_kernel(r_ref, v_ref, kap_ref, lw_ref, kd_ref, b_ref, o_ref, st_ref):
    d = pl.program_id(2)
    i = pl.program_id(3)
    tm = SEQ_TILE
    nck = tm // CHUNK

    @pl.when(i == 0)
    def _():
        st_ref[...] = jnp.zeros_like(st_ref)

    fwd = d == 0
    R = r_ref[0]
    V = v_ref[0]
    KAP = kap_ref[0]
    LW = lw_ref[0, 0]
    KD = kd_ref[0, 0]
    BV = b_ref[0, 0]

    ri = lax.broadcasted_iota(jnp.int32, (tm, tm), 0)
    ci = lax.broadcasted_iota(jnp.int32, (tm, tm), 1)
    same = (ri // CHUNK) == (ci // CHUNK)
    ahead = jnp.where(fwd, ri - ci, ci - ri)
    strict = same & (ahead > 0)
    incl = same & (ahead >= 0)

    g = _ones_mm(jnp.where(incl, 1.0, 0.0).astype(BF16), LW)
    eg = jnp.exp(g)
    egx = jnp.exp(g - LW)
    eng = jnp.exp(-g)
    At = -KAP * egx
    Rt = R * eg
    Bb = BV * eng
    Kb = KD * eng
    RR = jnp.concatenate([Bb, Kb], axis=0)

    lane = lax.broadcasted_iota(jnp.int32, (1, LANES), 1)
    Wt = jnp.zeros((tm, LANES), F32)
    Ut = jnp.zeros((tm, LANES), F32)
    Rh = jnp.zeros((tm, LANES), F32)
    Oh = jnp.zeros((tm, LANES), F32)
    for h in range(2):
        mh = jnp.where((lane // HEAD_DIM) == h, 1.0, 0.0)
        Ath = At * mh
        Rth = Rt * mh
        Vh = V * mh
        Q = _mm3_nt(jnp.concatenate([Ath, Rth], axis=0), RR)
        N = jnp.where(strict, Q[:tm, :tm], 0.0)
        Aak = jnp.where(strict, Q[:tm, tm:], 0.0)
        Arb = jnp.where(incl, Q[tm:, :tm], 0.0)
        Ark = jnp.where(incl, Q[tm:, tm:], 0.0)
        X = jnp.concatenate([Ath, _mm3(Aak, Vh)], axis=1)
        X = X + _mm3(N, X)
        P = N
        for _ in range(5):
            P = _mm3(P, P)
            X = X + _mm3(P, X)
        Wh = X[:, :LANES]
        Uh = X[:, LANES:]
        Wt = Wt + Wh
        Ut = Ut + Uh
        Rh = Rh + Rth + _mm3(Arb, Wh)
        Oh = Oh + _mm3(Arb, Uh) + _mm3(Ark, Vh)

    ends = []
    for c in range(nck):
        e_f = g[c * CHUNK + CHUNK - 1:c * CHUNK + CHUNK, :]
        e_b = g[c * CHUNK:c * CHUNK + 1, :]
        ends.append(jnp.where(fwd, e_f, e_b))
    gend = jnp.concatenate([jnp.broadcast_to(e, (CHUNK, LANES)) for e in ends], axis=0)
    egc = jnp.exp(gend - g)
    BhT = (BV * egc).T
    KhT = (KD * egc).T
    r128 = lax.broadcasted_iota(jnp.int32, (LANES, LANES), 0)
    c128 = lax.broadcasted_iota(jnp.int32, (LANES, LANES), 1)
    bdm = (r128 // HEAD_DIM) == (c128 // HEAD_DIM)
    eye = r128 == c128
    colc = lax.broadcasted_iota(jnp.int32, (1, tm), 1) // CHUNK
    Gs, Hs = [], []
    for c in range(nck):
        cm = jnp.where(colc == c, 1.0, 0.0)
        Bc = BhT * cm
        Kc = KhT * cm
        Gc = jnp.where(eye, jnp.exp(ends[c]), 0.0) + jnp.where(bdm, _mm3(Bc, Wt), 0.0)
        Hc = jnp.where(bdm, _mm3(Bc, Ut) + _mm3(Kc, V), 0.0)
        Gs.append(Gc)
        Hs.append(Hc)

    St = st_ref[...]
    outs = [None] * nck
    for step in range(nck):
        c_f = step
        c_b = nck - 1 - step
        rows_f = slice(c_f * CHUNK, (c_f + 1) * CHUNK)
        rows_b = slice(c_b * CHUNK, (c_b + 1) * CHUNK)
        Rc = jnp.where(fwd, Rh[rows_f], Rh[rows_b])
        Oc = jnp.where(fwd, Oh[rows_f], Oh[rows_b])
        Gc = jnp.where(fwd, Gs[c_f], Gs[c_b])
        Hc = jnp.where(fwd, Hs[c_f], Hs[c_b])
        outs[step] = _mm3(Rc, St) + Oc
        St = _mm3(Gc, St) + Hc
    st_ref[...] = St
    o_f = jnp.concatenate(outs, axis=0)
    o_b = jnp.concatenate(outs[::-1], axis=0)
    o_ref[0, 0] = jnp.where(fwd, o_f, o_b)


def _rwkv_scan(r, v, kap, lw, kd, b):
    B, S, _ = r.shape
    tm = SEQ_TILE
    n = S // tm
    pos = lambda d, i: jnp.where(d == 0, i, n - 1 - i)
    tok = pl.BlockSpec((1, tm, LANES), lambda bb, hp, d, i: (bb, pos(d, i), hp))
    tok2 = pl.BlockSpec((1, 1, tm, LANES), lambda bb, hp, d, i: (d, bb, pos(d, i), hp))
    return pl.pallas_call(
        _rwkv_scan_kernel,
        name="rwkv_scan",
        grid=(B, WIDTH // LANES, 2, n),
        in_specs=[tok, tok, tok, tok2, tok2, tok2],
        out_specs=tok2,
        out_shape=jax.ShapeDtypeStruct((2, B, S, WIDTH), F32),
        scratch_shapes=[pltpu.VMEM((LANES, LANES), F32)],
        compiler_params=_cparams(("parallel", "parallel", "parallel", "arbitrary")),
    )(r, v, kap, lw, kd, b)


def _retention_kernel(q_ref, k_ref, v_ref, cos_ref, sin_ref, lg_ref, o_ref, st_ref):
    hp = pl.program_id(1)
    d = pl.program_id(2)
    i = pl.program_id(3)
    tm = SEQ_TILE

    @pl.when(i == 0)
    def _():
        st_ref[...] = jnp.zeros_like(st_ref)

    fwd = d == 0
    lane = lax.broadcasted_iota(jnp.int32, (1, LANES), 1)
    lg0 = lg_ref[pl.ds(2 * hp, 1), :]
    lg1 = lg_ref[pl.ds(2 * hp + 1, 1), :]
    lgl = jnp.where(lane < HEAD_DIM, lg0, lg1)

    cosv = cos_ref[...]
    sinv = sin_ref[...]
    half = HEAD_DIM // 2
    first = (lane % HEAD_DIM) < half

    def rot(t):
        sw = jnp.where(first, pltpu.roll(t, LANES - half, 1), pltpu.roll(t, half, 1))
        return t * cosv + sw * sinv

    q = rot(q_ref[0]) * (HEAD_DIM ** -0.5)
    k = rot(k_ref[0])
    v = v_ref[0]

    rowf = lax.broadcasted_iota(jnp.int32, (tm, LANES), 0).astype(F32)
    e_q = jnp.where(fwd, rowf + 1.0, tm - rowf)
    e_k = jnp.where(fwd, tm - 1.0 - rowf, rowf)
    St = st_ref[...]
    cross = _dot((q * jnp.exp(e_q * lgl)).astype(BF16), St.astype(BF16))

    kz = (k * jnp.exp(e_k * lgl)).astype(BF16)
    r128 = lax.broadcasted_iota(jnp.int32, (LANES, LANES), 0)
    c128 = lax.broadcasted_iota(jnp.int32, (LANES, LANES), 1)
    bdm = (r128 // HEAD_DIM) == (c128 // HEAD_DIM)
    kv = lax.dot_general(kz, v.astype(BF16), (((0,), (0,)), ((), ())), preferred_element_type=F32)
    st_ref[...] = jnp.exp(tm * lgl) * St + jnp.where(bdm, kv, 0.0)

    @pl.when(fwd)
    def _():
        ri = lax.broadcasted_iota(jnp.int32, (tm, tm), 0)
        ci = lax.broadcasted_iota(jnp.int32, (tm, tm), 1)
        dist = jnp.abs(ri - ci).astype(F32)
        acc = cross
        kb = k.astype(BF16)
        for h in range(2):
            mh = jnp.where((lane // HEAD_DIM) == h, 1.0, 0.0)
            lgh = lg0 if h == 0 else lg1
            s = lax.dot_general((q * mh).astype(BF16), kb, _NT, preferred_element_type=F32)
            s = s * jnp.exp(dist * lgh[:, 0:1])
            acc = acc + _dot(s.astype(BF16), (v * mh).astype(BF16))
        o_ref[0, 0] = acc

    @pl.when(jnp.logical_not(fwd))
    def _():
        o_ref[0, 0] = cross


def _retention(p_ret, cos_t, sin_t, lg):
    B, S, _ = p_ret.shape
    tm = SEQ_TILE
    n = S // tm
    nhp = WIDTH // LANES
    pos = lambda d, i: jnp.where(d == 0, i, n - 1 - i)
    col = lambda off: pl.BlockSpec((1, tm, LANES), lambda bb, hp, d, i: (bb, pos(d, i), off + hp))
    tab = pl.BlockSpec((tm, LANES), lambda bb, hp, d, i: (pos(d, i), 0))
    return pl.pallas_call(
        _retention_kernel,
        name="retention",
        grid=(B, nhp, 2, n),
        in_specs=[col(0), col(nhp), col(2 * nhp), tab, tab,
                  pl.BlockSpec((N_HEADS, LANES), lambda bb, hp, d, i: (0, 0))],
        out_specs=pl.BlockSpec((1, 1, tm, LANES), lambda bb, hp, d, i: (d, bb, pos(d, i), hp)),
        out_shape=jax.ShapeDtypeStruct((2, B, S, WIDTH), F32),
        scratch_shapes=[pltpu.VMEM((LANES, LANES), F32)],
        compiler_params=_cparams(("parallel", "parallel", "parallel", "arbitrary")),
    )(p_ret, p_ret, p_ret, cos_t, sin_t, lg)


def _head_norm(o, bd, eps):
    mu = _segsum(o, bd) * (1.0 / HEAD_DIM)
    dlt = o - mu
    var = _segsum(dlt * dlt, bd) * (1.0 / HEAD_DIM)
    return dlt * lax.rsqrt(var + eps)


def _merge_kernel(orw_ref, bonus_ref, g_ref, oret_ref, qg_ref, gate_ref, x_ref,
                  lng_ref, lnb_ref, rlng_ref, bd_ref, pa_ref, pb_ref, wo_ref, n2_ref, wq_ref, sk_ref,
                  x1_ref, h2_ref, sc_ref):
    bd = bd_ref[...]
    o = _head_norm(orw_ref[0] + orw_ref[1], bd, RWKV_GN_EPS) * lng_ref[...] + lnb_ref[...]
    o_a = (o + bonus_ref[...]) * g_ref[...]
    o_b = _head_norm(oret_ref[0] + oret_ref[1], bd, RET_GN_EPS) * rlng_ref[...] * jax.nn.silu(qg_ref[...])
    y_a = _dot(o_a.astype(BF16), pa_ref[...])
    y_b = _dot(o_b.astype(BF16), pb_ref[...])
    gate = gate_ref[...]
    mixed = jax.nn.sigmoid(gate[:, :D_MODEL]) * y_a + jax.nn.sigmoid(gate[:, D_MODEL:]) * y_b
    x1 = x_ref[...] + _dot(mixed.astype(BF16), wo_ref[...])
    x1_ref[...] = x1
    h2 = _rmsnorm(x1, n2_ref[...])
    hb = h2.astype(BF16)
    h2_ref[...] = hb
    q = _dot(hb, wq_ref[...]).astype(BF16)
    for j in range(2 * PEER_HEADS):
        sc_ref[j] = lax.dot_general(sk_ref[j], q[:, j * 128:(j + 1) * 128], _NT, preferred_element_type=F32)


def _merge(orw, bonus, g, oret, p_ret, p_gate, x2, lng, lnb, rlng, bd, pa, pb, wo, n2, wq, sk):
    T = x2.shape[0]
    tm = TOK_TILE
    full = lambda shape: pl.BlockSpec(shape, lambda i: (0,) * len(shape))
    tok = lambda w: pl.BlockSpec((tm, w), lambda i: (i, 0))
    two = pl.BlockSpec((2, tm, WIDTH), lambda i: (0, i, 0))
    return pl.pallas_call(
        _merge_kernel,
        name="merge",
        grid=(T // tm,),
        in_specs=[two, tok(WIDTH), tok(WIDTH), two,
                  pl.BlockSpec((tm, WIDTH), lambda i: (i, 3)),
                  tok(GATE_COLS), tok(D_MODEL),
                  full((1, WIDTH)), full((1, WIDTH)), full((1, WIDTH)), full((WIDTH, WIDTH)),
                  full((WIDTH, D_MODEL)), full((WIDTH, D_MODEL)), full((D_MODEL, D_MODEL)), full((1, D_MODEL)),
                  full((D_MODEL, 2 * PEER_HEADS * 128)), full((2 * PEER_HEADS, 128, 128))],
        out_specs=[tok(D_MODEL), tok(D_MODEL),
                   pl.BlockSpec((2 * PEER_HEADS, PEER_NKEYS, tm), lambda i: (0, 0, i))],
        out_shape=[jax.ShapeDtypeStruct((T, D_MODEL), F32),
                   jax.ShapeDtypeStruct((T, D_MODEL), BF16),
                   jax.ShapeDtypeStruct((2 * PEER_HEADS, PEER_NKEYS, T), F32)],
        compiler_params=_cparams(("parallel",)),
    )(orw, bonus, g, oret, p_ret, p_gate, x2, lng, lnb, rlng, bd, pa, pb, wo, n2, wq, sk)


def _top16(s):
    rows16 = lax.broadcasted_iota(jnp.int32, (PEER_TOPK, s.shape[1]), 0)
    vals = jnp.zeros((PEER_TOPK, s.shape[1]), F32)
    for a in range(PEER_TOPK):
        m = jnp.max(s, axis=0, keepdims=True)
        vals = jnp.where(rows16 == a, m, vals)
        s = jnp.where(s == m, NEG_BIG, s)
    return vals


def _topk_prep_kernel(sc_ref, c_ref, e1_ref, rk_ref, e2_ref):
    s1 = sc_ref[0]
    s2 = sc_ref[1]
    v1 = _top16(s1)
    v2 = _top16(s2)
    tl = s1.shape[1]
    cands = [v1[0:1, :] + v2]
    for a in range(1, 8):
        cands.append(v1[a:a + 1, :] + v2[0:8, :])
    cands.append(v1[8:16, :] + v2[0:1, :])
    tau = _top16(jnp.concatenate(cands, axis=0))[PEER_TOPK - 1:PEER_TOPK, :]

    cut = jnp.zeros_like(s1)
    rank2 = jnp.zeros_like(s2)
    cut16 = jnp.zeros_like(v1)
    for b in range(PEER_TOPK):
        vb = v2[b:b + 1, :]
        cut = cut + jnp.where(s1 + vb >= tau, 1.0, 0.0)
        rank2 = rank2 + jnp.where(vb > s2, 1.0, 0.0)
        cut16 = cut16 + jnp.where(v1 + vb >= tau, 1.0, 0.0)
    e1v = jnp.exp(v1 - v1[0:1, :])
    e2v = jnp.exp(v2 - v2[0:1, :])
    zacc = jnp.zeros_like(v1)
    for b in range(PEER_TOPK):
        zacc = zacc + jnp.where(cut16 > b, e1v, 0.0) * e2v[b:b + 1, :]
    z = jnp.sum(zacc, axis=0, keepdims=True)
    c_ref[0] = cut
    e1_ref[0] = jnp.exp(s1 - v1[0:1, :]) / z
    rk_ref[0] = rank2.astype(BF16)
    e2_ref[0] = jnp.exp(s2 - v2[0:1, :]).astype(BF16)
    del tl


def _topk_prep(sc):
    _, nk, T = sc.shape
    tl = TOPK_TILE
    blk = pl.BlockSpec((1, nk, tl), lambda h, t: (h, 0, t))
    sf = jax.ShapeDtypeStruct((PEER_HEADS, nk, T), F32)
    sb = jax.ShapeDtypeStruct((PEER_HEADS, nk, T), BF16)
    return pl.pallas_call(
        _topk_prep_kernel,
        name="topk_prep",
        grid=(PEER_HEADS, T // tl),
        in_specs=[pl.BlockSpec((2, nk, tl), lambda h, t: (h, 0, t))],
        out_specs=[blk, blk, blk, blk],
        out_shape=[sf, sf, sb, sb],
        compiler_params=_cparams(("parallel", "parallel")),
    )(sc)


def _gelu(x):
    return 0.5 * x * (1.0 + lax.erf(x * (2.0 ** -0.5)))


def _peer_kernel(h_ref, u_ref, vt_ref, c_ref, e1_ref, rk_ref, e2_ref, x1_ref, nf_ref, y_ref,
                 acc_ref, act_ref, p_ref):
    e = pl.program_id(1)
    ne = pl.num_programs(1)
    nrow = PEER_EXP_TILE // PEER_NKEYS

    @pl.when(e == 0)
    def _():
        acc_ref[...] = jnp.zeros_like(acc_ref)

    act_ref[...] = lax.dot_general(u_ref[...], h_ref[...], _NT, preferred_element_type=F32)
    base = pl.multiple_of(e * nrow, nrow)
    cuts = [c_ref[h, pl.ds(base, nrow), :] for h in range(PEER_HEADS)]
    e1s = [e1_ref[h, pl.ds(base, nrow), :] for h in range(PEER_HEADS)]
    for il in range(nrow):
        rows = slice(il * PEER_NKEYS, (il + 1) * PEER_NKEYS)
        w = None
        for h in range(PEER_HEADS):
            cut = cuts[h][il:il + 1, :].astype(BF16)
            e1 = e1s[h][il:il + 1, :].astype(BF16)
            term = jnp.where(rk_ref[h] < cut, e2_ref[h], jnp.zeros((), BF16)) * e1
            w = term if w is None else w + term
        p_ref[rows, :] = (w.astype(F32) * _gelu(act_ref[rows, :])).astype(BF16)
    acc_ref[...] += _dot(vt_ref[...], p_ref[...])

    @pl.when(e == ne - 1)
    def _():
        y = x1_ref[...] + acc_ref[...].T
        y_ref[...] = _rmsnorm(y, nf_ref[...])


def _peer(h2, u_bf16, vt_bf16, cut, e1, rank2, e2, x1, nf):
    T = h2.shape[0]
    tT = PEER_TOK_TILE
    eT = PEER_EXP_TILE
    sel = pl.BlockSpec((PEER_HEADS, PEER_NKEYS, tT), lambda t, e: (0, 0, t))
    return pl.pallas_call(
        _peer_kernel,
        name="peer",
        grid=(T // tT, PEER_EXPERTS // eT),
        in_specs=[pl.BlockSpec((tT, D_MODEL), lambda t, e: (t, 0)),
                  pl.BlockSpec((eT, D_MODEL), lambda t, e: (e, 0)),
                  pl.BlockSpec((D_MODEL, eT), lambda t, e: (0, e)),
                  sel, sel, sel, sel,
                  pl.BlockSpec((tT, D_MODEL), lambda t, e: (t, 0)),
                  pl.BlockSpec((1, D_MODEL), lambda t, e: (0, 0))],
        out_specs=pl.BlockSpec((tT, D_MODEL), lambda t, e: (t, 0)),
        out_shape=jax.ShapeDtypeStruct((T, D_MODEL), F32),
        scratch_shapes=[pltpu.VMEM((D_MODEL, tT), F32),
                        pltpu.VMEM((eT, tT), F32),
                        pltpu.VMEM((eT, tT), BF16)],
        compiler_params=_cparams(("parallel", "arbitrary")),
    )(h2, u_bf16, vt_bf16, cut, e1, rank2, e2, x1, nf)


def _rope_tables(S):
    half = HEAD_DIM // 2
    inv = ROPE_BASE ** (-jnp.arange(half, dtype=F32) / half)
    ang = jnp.arange(S, dtype=F32)[:, None] * inv[None, :]
    c = jnp.cos(ang)
    s = jnp.sin(ang)
    reps = LANES // HEAD_DIM
    cos_t = jnp.tile(jnp.concatenate([c, c], axis=1), (1, reps))
    sin_t = jnp.tile(jnp.concatenate([-s, s], axis=1), (1, reps))
    return cos_t, sin_t


def _trunk(x, w):
    B, S, _ = x.shape
    T = B * S
    x2 = x.reshape(T, D_MODEL)
    p_rwkv, p_ret, p_gate = _inproj(x2, w["norm1_g"], w["w_in"])
    r, v, kap, lw, kd, b, bonus, g = _rwkv_prep(
        p_rwkv.reshape(B, S, RWKV_COLS), w["mu"], w["w0"], w["w2"], w["a0"], w["a2"], w["g2"],
        w["k_k"], w["k_a"], w["r_k"], w["bd"])
    orw = _rwkv_scan(r, v, kap, lw, kd, b)
    cos_t, sin_t = _rope_tables(S)
    oret = _retention(p_ret.reshape(B, S, RET_COLS), cos_t, sin_t, w["lg"])
    x1, h2, sc = _merge(orw.reshape(2, T, WIDTH), bonus.reshape(T, WIDTH), g.reshape(T, WIDTH),
                        oret.reshape(2, T, WIDTH), p_ret, p_gate, x2,
                        w["ln_g"], w["ln_b"], w["ret_ln_g"], w["bd"], w["proj_a"], w["proj_b"], w["w_out"],
                        w["norm2_g"], w["wq"], w["sk"])
    cut, e1, rank2, e2 = _topk_prep(sc)
    y = _peer(h2, w["u"], w["vt"], cut, e1, rank2, e2, x1, w["normf_g"])
    return y.reshape(B, S, D_MODEL)


def kernel(x_prompt, x_sample, norm1_g, w_in, rwkv_mu, rwkv_w0, rwkv_w2, rwkv_a0, rwkv_a2, rwkv_g2, rwkv_k_k,
           rwkv_k_a, rwkv_r_k, rwkv_ln_g, rwkv_ln_b, ret_ln_g, proj_a, proj_b, w_out, norm2_g, peer_wq,
           peer_subkeys, peer_u, peer_v, normf_g):
    assert norm1_g.shape[0] == 1, "single-layer trunk"
    head_id = jnp.arange(WIDTH) // HEAD_DIM
    w = {
        "norm1_g": norm1_g[0][None, :],
        "w_in": w_in[0].astype(BF16),
        "mu": rwkv_mu[0],
        "w0": rwkv_w0[0],
        "w2": rwkv_w2[0].astype(BF16),
        "a0": rwkv_a0[0],
        "a2": rwkv_a2[0].astype(BF16),
        "g2": rwkv_g2[0].astype(BF16),
        "k_k": rwkv_k_k[0][None, :],
        "k_a": rwkv_k_a[0][None, :],
        "r_k": rwkv_r_k[0].reshape(1, WIDTH),
        "ln_g": rwkv_ln_g[0][None, :],
        "ln_b": rwkv_ln_b[0][None, :],
        "ret_ln_g": ret_ln_g[0][None, :],
        "bd": (head_id[:, None] == head_id[None, :]).astype(BF16),
        "lg": jnp.broadcast_to(jnp.log1p(-jnp.exp2(-5.0 - jnp.arange(N_HEADS, dtype=F32)))[:, None],
                               (N_HEADS, LANES)),
        "proj_a": proj_a[0].astype(BF16),
        "proj_b": proj_b[0].astype(BF16),
        "w_out": w_out[0].astype(BF16),
        "norm2_g": norm2_g[0][None, :],
        "wq": peer_wq[0].astype(BF16),
        "sk": peer_subkeys[0].reshape(2 * PEER_HEADS, PEER_NKEYS, 128).astype(BF16),
        "u": peer_u[0].astype(BF16),
        "vt": peer_v[0].T.astype(BF16),
        "normf_g": normf_g[None, :],
    }
    return (_trunk(x_prompt, w), _trunk(x_sample, w))
```

```python
import functools

import jax
import jax.numpy as jnp
from jax import lax
from jax.experimental import pallas as pl
from jax.experimental.pallas import tpu as pltpu

F32 = jnp.float32
BF16 = jnp.bfloat16

D_MODEL = 1024
HEAD_DIM = 64
N_HEADS = 8
WIDTH = N_HEADS * HEAD_DIM
RWKV_COLS = 1920
RET_COLS = 2048
GATE_COLS = 2048
RWKV_GN_EPS = 64e-5
RET_GN_EPS = 1e-6
NORM_EPS = 1e-6
ROPE_BASE = 10000.0
PEER_HEADS = 8
PEER_NKEYS = 128
PEER_TOPK = 16
PEER_EXPERTS = PEER_NKEYS * PEER_NKEYS

LANES = 128
CHUNK = 64
SEQ_TILE = 256
TOK_TILE = 256
PEER_TOK_TILE = 512
PEER_EXP_TILE = 1024
TOPK_TILE = 512
VMEM_LIMIT = 56 * 1024 * 1024
NEG_BIG = -1e30

_NT = (((1,), (1,)), ((), ()))


def _cparams(sem):
    return pltpu.CompilerParams(dimension_semantics=sem, vmem_limit_bytes=VMEM_LIMIT)


def _dot(a, b):
    return jnp.dot(a, b, preferred_element_type=F32)


def _dot_nt(a, b):
    return lax.dot_general(a, b, _NT, preferred_element_type=F32)


def _dot_tn(a, b):
    return lax.dot_general(a, b, (((0,), (0,)), ((), ())), preferred_element_type=F32)


def _split2(x):
    hi = x.astype(BF16)
    lo = (x - hi.astype(F32)).astype(BF16)
    return hi, lo


def _ones_mm(m_bf16, x):
    x0 = x.astype(BF16)
    r1 = x - x0.astype(F32)
    x1 = r1.astype(BF16)
    x2 = (r1 - x1.astype(F32)).astype(BF16)
    return _dot(m_bf16, x0) + (_dot(m_bf16, x1) + _dot(m_bf16, x2))


def _segsum(x, bd):
    hi, lo = _split2(x)
    return _dot(hi, bd) + _dot(lo, bd)


def _rmsnorm(x, g):
    return x * lax.rsqrt(jnp.mean(x * x, axis=-1, keepdims=True) + NORM_EPS) * g


def _inproj_kernel(x_ref, g_ref, w_ref, o1_ref, o2_ref, o3_ref):
    hb = _rmsnorm(x_ref[...], g_ref[...]).astype(BF16)
    o1_ref[...] = _dot(hb, w_ref[:, 0:RWKV_COLS])
    o2_ref[...] = _dot(hb, w_ref[:, RWKV_COLS:RWKV_COLS + RET_COLS])
    o3_ref[...] = _dot(hb, w_ref[:, RWKV_COLS + RET_COLS:])


def _inproj(x2, g, w_bf16):
    T = x2.shape[0]
    tm = TOK_TILE
    ncol = w_bf16.shape[1]
    return pl.pallas_call(
        _inproj_kernel,
        name="inproj",
        grid=(T // tm,),
        in_specs=[
            pl.BlockSpec((tm, D_MODEL), lambda i: (i, 0)),
            pl.BlockSpec((1, D_MODEL), lambda i: (0, 0)),
            pl.BlockSpec((D_MODEL, ncol), lambda i: (0, 0)),
        ],
        out_specs=[
            pl.BlockSpec((tm, RWKV_COLS), lambda i: (i, 0)),
            pl.BlockSpec((tm, RET_COLS), lambda i: (i, 0)),
            pl.BlockSpec((tm, GATE_COLS), lambda i: (i, 0)),
        ],
        out_shape=[
            jax.ShapeDtypeStruct((T, RWKV_COLS), F32),
            jax.ShapeDtypeStruct((T, RET_COLS), F32),
            jax.ShapeDtypeStruct((T, GATE_COLS), F32),
        ],
        compiler_params=_cparams(("parallel",)),
    )(x2, g, w_bf16)


def _softplus(y):
    return jnp.maximum(y, 0.0) + jnp.log1p(jnp.exp(-jnp.abs(y)))


def _rwkv_prep_kernel(p_ref, pprev_ref, pnext_ref, mu_ref, w0_ref, w2_ref, a0_ref, a2_ref, g2_ref,
                      kk_ref, ka_ref, rk_ref, bd_ref,
                      r_ref, v_ref, kap_ref, lw_ref, kd_ref, b_ref, bonus_ref, g_ref):
    i = pl.program_id(1)
    n = pl.num_programs(1)
    p = p_ref[0]
    tm = p.shape[0]
    prev_row = jnp.where(i > 0, pprev_ref[0, 7:8, :], 0.0)
    next_row = jnp.where(i < n - 1, pnext_ref[0, 0:1, :], 0.0)
    row = lax.broadcasted_iota(jnp.int32, p.shape, 0)
    prev = jnp.where(row == 0, prev_row, pltpu.roll(p, 1, 0))
    nxt = jnp.where(row == tm - 1, next_row, pltpu.roll(p, tm - 1, 0))
    ps = p + mu_ref[0:1, :] * (prev - p) + mu_ref[1:2, :] * (nxt - p)

    bd = bd_ref[...]
    r = ps[:, 0:WIDTH]
    k = ps[:, WIDTH:2 * WIDTH]
    v = ps[:, 2 * WIDTH:3 * WIDTH]
    kk = k * kk_ref[...]
    kappa = kk / jnp.maximum(jnp.sqrt(_segsum(kk * kk, bd)), 1e-12)
    r_ref[0] = r
    v_ref[0] = v
    kap_ref[0] = kappa
    ksum = jnp.zeros_like(k)
    for d in range(2):
        wl = ps[:, 1536 + 64 * d:1600 + 64 * d]
        al = ps[:, 1664 + 64 * d:1728 + 64 * d]
        z = w0_ref[d:d + 1, :] + _dot(jnp.tanh(wl).astype(BF16), w2_ref[d])
        w_log = -_softplus(-z) - 0.5
        lw_ref[d, 0] = -jnp.exp(w_log)
        a = jax.nn.sigmoid(a0_ref[d:d + 1, :] + _dot(al.astype(BF16), a2_ref[d]))
        kd = k * (1.0 + (a - 1.0) * ka_ref[...])
        kd_ref[d, 0] = kd
        b_ref[d, 0] = kappa * a
        ksum = ksum + kd
    bonus_ref[0] = _segsum(r * (0.5 * ksum) * rk_ref[...], bd) * v
    g_ref[0] = _dot(jax.nn.sigmoid(ps[:, 1792:1920]).astype(BF16), g2_ref[...])


def _rwkv_prep(p_rwkv, mu, w0, w2, a0, a2, g2, k_k, k_a, r_k, bd):
    B, S, _ = p_rwkv.shape
    tm = SEQ_TILE
    n = S // tm
    nb8 = S // 8
    full = lambda shape: pl.BlockSpec(shape, lambda b, i: (0,) * len(shape))
    tok = pl.BlockSpec((1, tm, WIDTH), lambda b, i: (b, i, 0))
    tok2 = pl.BlockSpec((2, 1, tm, WIDTH), lambda b, i: (0, b, i, 0))
    s1 = jax.ShapeDtypeStruct((B, S, WIDTH), F32)
    s2 = jax.ShapeDtypeStruct((2, B, S, WIDTH), F32)
    return pl.pallas_call(
        _rwkv_prep_kernel,
        name="rwkv_prep",
        grid=(B, n),
        in_specs=[
            pl.BlockSpec((1, tm, RWKV_COLS), lambda b, i: (b, i, 0)),
            pl.BlockSpec((1, 8, RWKV_COLS), lambda b, i: (b, jnp.maximum(i * (tm // 8) - 1, 0), 0)),
            pl.BlockSpec((1, 8, RWKV_COLS), lambda b, i: (b, jnp.minimum((i + 1) * (tm // 8), nb8 - 1), 0)),
            full((2, RWKV_COLS)), full((2, WIDTH)), full((2, 64, WIDTH)), full((2, WIDTH)),
            full((2, 64, WIDTH)), full((128, WIDTH)), full((1, WIDTH)), full((1, WIDTH)), full((1, WIDTH)),
            full((WIDTH, WIDTH)),
        ],
        out_specs=[tok, tok, tok, tok2, tok2, tok2, tok, tok],
        out_shape=[s1, s1, s1, s2, s2, s2, s1, s1],
        compiler_params=_cparams(("parallel", "parallel")),
    )(p_rwkv, p_rwkv, p_rwkv, mu, w0, w2, a0, a2, g2, k_k, k_a, r_k, bd)


def _rwkv_scan_kernel(r_ref, v_ref, kap_ref, lw_ref, kd_ref, b_ref, o_ref, st_ref):
    d = pl.program_id(2)
    i = pl.program_id(3)
    tm = SEQ_TILE
    nck = tm // CHUNK

    @pl.when(i == 0)
    def _():
        st_ref[...] = jnp.zeros_like(st_ref)

    fwd = d == 0
    R = r_ref[0]
    V = v_ref[0]
    KAP = kap_ref[0]
    LW = lw_ref[0, 0]
    KD = kd_ref[0, 0]
    BV = b_ref[0, 0]

    ri = lax.broadcasted_iota(jnp.int32, (tm, tm), 0)
    ci = lax.broadcasted_iota(jnp.int32, (tm, tm), 1)
    same = (ri // CHUNK) == (ci // CHUNK)
    ahead = jnp.where(fwd, ri - ci, ci - ri)
    strict = same & (ahead > 0)
    incl = same & (ahead >= 0)

    g = _ones_mm(jnp.where(incl, 1.0, 0.0).astype(BF16), LW)
    eg = jnp.exp(g)
    egx = jnp.exp(g - LW)
    eng = jnp.exp(-g)
    At = -KAP * egx
    Rt = R * eg
    Bb = BV * eng
    Kb = KD * eng
    RRb = jnp.concatenate([Bb, Kb], axis=0).astype(BF16)

    lane = lax.broadcasted_iota(jnp.int32, (1, LANES), 1)
    Wt = jnp.zeros((tm, LANES), F32)
    Ut = jnp.zeros((tm, LANES), F32)
    Rh = jnp.zeros((tm, LANES), F32)
    Oh = jnp.zeros((tm, LANES), F32)
    for h in range(2):
        mh = jnp.where((lane // HEAD_DIM) == h, 1.0, 0.0)
        Ath = At * mh
        Rth = Rt * mh
        Vhb = (V * mh).astype(BF16)
        Q = _dot_nt(jnp.concatenate([Ath, Rth], axis=0).astype(BF16), RRb)
        Nb = jnp.where(strict, Q[:tm, :tm], 0.0).astype(BF16)
        Aak = jnp.where(strict, Q[:tm, tm:], 0.0).astype(BF16)
        Arb = jnp.where(incl, Q[tm:, :tm], 0.0).astype(BF16)
        Ark = jnp.where(incl, Q[tm:, tm:], 0.0).astype(BF16)
        X = jnp.concatenate([Ath, _dot(Aak, Vhb)], axis=1)
        X = X + _dot(Nb, X.astype(BF16))
        Pb = Nb
        for _ in range(5):
            Pb = _dot(Pb, Pb).astype(BF16)
            X = X + _dot(Pb, X.astype(BF16))
        Wh = X[:, :LANES]
        Uh = X[:, LANES:]
        Wt = Wt + Wh
        Ut = Ut + Uh
        Rh = Rh + Rth + _dot(Arb, Wh.astype(BF16))
        Oh = Oh + _dot(Arb, Uh.astype(BF16)) + _dot(Ark, Vhb)

    ends = []
    for c in range(nck):
        e_f = g[c * CHUNK + CHUNK - 1:c * CHUNK + CHUNK, :]
        e_b = g[c * CHUNK:c * CHUNK + 1, :]
        ends.append(jnp.where(fwd, e_f, e_b))
    gend = jnp.concatenate([jnp.broadcast_to(e, (CHUNK, LANES)) for e in ends], axis=0)
    egc = jnp.exp(gend - g)
    BhT = (BV * egc).T.astype(BF16)
    KhT = (KD * egc).T.astype(BF16)
    Wtb = Wt.astype(BF16)
    Utb = Ut.astype(BF16)
    Vb = V.astype(BF16)
    r128 = lax.broadcasted_iota(jnp.int32, (LANES, LANES), 0)
    c128 = lax.broadcasted_iota(jnp.int32, (LANES, LANES), 1)
    bdm = (r128 // HEAD_DIM) == (c128 // HEAD_DIM)
    eye = r128 == c128
    colc = lax.broadcasted_iota(jnp.int32, (1, tm), 1) // CHUNK
    Gs, Hs = [], []
    for c in range(nck):
        cm = colc == c
        Bc = jnp.where(cm, BhT, jnp.zeros((), BF16))
        Kc = jnp.where(cm, KhT, jnp.zeros((), BF16))
        Gc = jnp.where(eye, jnp.exp(ends[c]), 0.0) + jnp.where(bdm, _dot(Bc, Wtb), 0.0)
        Hc = jnp.where(bdm, _dot(Bc, Utb) + _dot(Kc, Vb), 0.0)
        Gs.append(Gc)
        Hs.append(Hc)

    St = st_ref[...]
    outs = [None] * nck
    for step in range(nck):
        c_f = step
        c_b = nck - 1 - step
        rows_f = slice(c_f * CHUNK, (c_f + 1) * CHUNK)
        rows_b = slice(c_b * CHUNK, (c_b + 1) * CHUNK)
        Rc = jnp.where(fwd, Rh[rows_f], Rh[rows_b])
        Oc = jnp.where(fwd, Oh[rows_f], Oh[rows_b])
        Gc = jnp.where(fwd, Gs[c_f], Gs[c_b])
        Hc = jnp.where(fwd, Hs[c_f], Hs[c_b])
        Stb = St.astype(BF16)
        outs[step] = _dot(Rc.astype(BF16), Stb) + Oc
        St = _dot(Gc.astype(BF16), Stb) + Hc
    st_ref[...] = St
    o_f = jnp.concatenate(outs, axis=0)
    o_b = jnp.concatenate(outs[::-1], axis=0)
    o_ref[0, 0] = jnp.where(fwd, o_f, o_b)


def _rwkv_scan(r, v, kap, lw, kd, b):
    B, S, _ = r.shape
    tm = SEQ_TILE
    n = S // tm
    pos = lambda d, i: jnp.where(d == 0, i, n - 1 - i)
    tok = pl.BlockSpec((1, tm, LANES), lambda bb, hp, d, i: (bb, pos(d, i), hp))
    tok2 = pl.BlockSpec((1, 1, tm, LANES), lambda bb, hp, d, i: (d, bb, pos(d, i), hp))
    return pl.pallas_call(
        _rwkv_scan_kernel,
        name="rwkv_scan",
        grid=(B, WIDTH // LANES, 2, n),
        in_specs=[tok, tok, tok, tok2, tok2, tok2],
        out_specs=tok2,
        out_shape=jax.ShapeDtypeStruct((2, B, S, WIDTH), F32),
        scratch_shapes=[pltpu.VMEM((LANES, LANES), F32)],
        compiler_params=_cparams(("parallel", "parallel", "parallel", "arbitrary")),
    )(r, v, kap, lw, kd, b)


def _retention_kernel(q_ref, k_ref, v_ref, cos_ref, sin_ref, lg_ref, o_ref, st_ref):
    hp = pl.program_id(1)
    d = pl.program_id(2)
    i = pl.program_id(3)
    tm = SEQ_TILE

    @pl.when(i == 0)
    def _():
        st_ref[...] = jnp.zeros_like(st_ref)

    fwd = d == 0
    lane = lax.broadcasted_iota(jnp.int32, (1, LANES), 1)
    lg0 = lg_ref[pl.ds(2 * hp, 1), :]
    lg1 = lg_ref[pl.ds(2 * hp + 1, 1), :]
    lgl = jnp.where(lane < HEAD_DIM, lg0, lg1)

    cosv = cos_ref[...]
    sinv = sin_ref[...]
    half = HEAD_DIM // 2
    first = (lane % HEAD_DIM) < half

    def rot(t):
        sw = jnp.where(first, pltpu.roll(t, LANES - half, 1), pltpu.roll(t, half, 1))
        return t * cosv + sw * sinv

    q = rot(q_ref[0]) * (HEAD_DIM ** -0.5)
    k = rot(k_ref[0])
    v = v_ref[0]

    rowf = lax.broadcasted_iota(jnp.int32, (tm, LANES), 0).astype(F32)
    e_q = jnp.where(fwd, rowf + 1.0, tm - rowf)
    e_k = jnp.where(fwd, tm - 1.0 - rowf, rowf)
    St = st_ref[...]
    cross = _dot((q * jnp.exp(e_q * lgl)).astype(BF16), St.astype(BF16))

    kz = (k * jnp.exp(e_k * lgl)).astype(BF16)
    r128 = lax.broadcasted_iota(jnp.int32, (LANES, LANES), 0)
    c128 = lax.broadcasted_iota(jnp.int32, (LANES, LANES), 1)
    bdm = (r128 // HEAD_DIM) == (c128 // HEAD_DIM)
    kv = _dot_tn(kz, v.astype(BF16))
    st_ref[...] = jnp.exp(tm * lgl) * St + jnp.where(bdm, kv, 0.0)

    @pl.when(fwd)
    def _():
        ri = lax.broadcasted_iota(jnp.int32, (tm, tm), 0)
        ci = lax.broadcasted_iota(jnp.int32, (tm, tm), 1)
        dist = jnp.abs(ri - ci).astype(F32)
        acc = cross
        kb = k.astype(BF16)
        for h in range(2):
            mh = jnp.where((lane // HEAD_DIM) == h, 1.0, 0.0)
            lgh = lg0 if h == 0 else lg1
            s = _dot_nt((q * mh).astype(BF16), kb)
            s = s * jnp.exp(dist * lgh[:, 0:1])
            acc = acc + _dot(s.astype(BF16), (v * mh).astype(BF16))
        o_ref[0, 0] = acc

    @pl.when(jnp.logical_not(fwd))
    def _():
        o_ref[0, 0] = cross


def _retention(p_ret, cos_t, sin_t, lg):
    B, S, _ = p_ret.shape
    tm = SEQ_TILE
    n = S // tm
    nhp = WIDTH // LANES
    pos = lambda d, i: jnp.where(d == 0, i, n - 1 - i)
    col = lambda off: pl.BlockSpec((1, tm, LANES), lambda bb, hp, d, i: (bb, pos(d, i), off + hp))
    tab = pl.BlockSpec((tm, LANES), lambda bb, hp, d, i: (pos(d, i), 0))
    return pl.pallas_call(
        _retention_kernel,
        name="retention",
        grid=(B, nhp, 2, n),
        in_specs=[col(0), col(nhp), col(2 * nhp), tab, tab,
                  pl.BlockSpec((N_HEADS, LANES), lambda bb, hp, d, i: (0, 0))],
        out_specs=pl.BlockSpec((1, 1, tm, LANES), lambda bb, hp, d, i: (d, bb, pos(d, i), hp)),
        out_shape=jax.ShapeDtypeStruct((2, B, S, WIDTH), F32),
        scratch_shapes=[pltpu.VMEM((LANES, LANES), F32)],
        compiler_params=_cparams(("parallel", "parallel", "parallel", "arbitrary")),
    )(p_ret, p_ret, p_ret, cos_t, sin_t, lg)


def _head_norm(o, bd, eps):
    mu = _segsum(o, bd) * (1.0 / HEAD_DIM)
    dlt = o - mu
    var = _segsum(dlt * dlt, bd) * (1.0 / HEAD_DIM)
    return dlt * lax.rsqrt(var + eps)


def _merge_kernel(orw_ref, bonus_ref, g_ref, oret_ref, qg_ref, gate_ref, x_ref,
                  lng_ref, lnb_ref, rlng_ref, bd_ref, pa_ref, pb_ref, wo_ref, n2_ref, wq_ref, sk_ref,
                  x1_ref, h2_ref, sc_ref):
    bd = bd_ref[...]
    o = _head_norm(orw_ref[0] + orw_ref[1], bd, RWKV_GN_EPS) * lng_ref[...] + lnb_ref[...]
    o_a = (o + bonus_ref[...]) * g_ref[...]
    o_b = _head_norm(oret_ref[0] + oret_ref[1], bd, RET_GN_EPS) * rlng_ref[...] * jax.nn.silu(qg_ref[...])
    y_a = _dot(o_a.astype(BF16), pa_ref[...])
    y_b = _dot(o_b.astype(BF16), pb_ref[...])
    gate = gate_ref[...]
    mixed = jax.nn.sigmoid(gate[:, :D_MODEL]) * y_a + jax.nn.sigmoid(gate[:, D_MODEL:]) * y_b
    x1 = x_ref[...] + _dot(mixed.astype(BF16), wo_ref[...])
    x1_ref[...] = x1
    h2 = _rmsnorm(x1, n2_ref[...])
    hb = h2.astype(BF16)
    h2_ref[...] = hb
    q = _dot(hb, wq_ref[...]).astype(BF16)
    for j in range(2 * PEER_HEADS):
        sc_ref[j] = _dot_nt(sk_ref[j], q[:, j * 128:(j + 1) * 128])


def _merge(orw, bonus, g, oret, p_ret, p_gate, x2, lng, lnb, rlng, bd, pa, pb, wo, n2, wq, sk):
    T = x2.shape[0]
    tm = TOK_TILE
    full = lambda shape: pl.BlockSpec(shape, lambda i: (0,) * len(shape))
    tok = lambda w: pl.BlockSpec((tm, w), lambda i: (i, 0))
    two = pl.BlockSpec((2, tm, WIDTH), lambda i: (0, i, 0))
    return pl.pallas_call(
        _merge_kernel,
        name="merge",
        grid=(T // tm,),
        in_specs=[two, tok(WIDTH), tok(WIDTH), two,
                  pl.BlockSpec((tm, WIDTH), lambda i: (i, 3)),
                  tok(GATE_COLS), tok(D_MODEL),
                  full((1, WIDTH)), full((1, WIDTH)), full((1, WIDTH)), full((WIDTH, WIDTH)),
                  full((WIDTH, D_MODEL)), full((WIDTH, D_MODEL)), full((D_MODEL, D_MODEL)), full((1, D_MODEL)),
                  full((D_MODEL, 2 * PEER_HEADS * 128)), full((2 * PEER_HEADS, 128, 128))],
        out_specs=[tok(D_MODEL), tok(D_MODEL),
                   pl.BlockSpec((2 * PEER_HEADS, PEER_NKEYS, tm), lambda i: (0, 0, i))],
        out_shape=[jax.ShapeDtypeStruct((T, D_MODEL), F32),
                   jax.ShapeDtypeStruct((T, D_MODEL), BF16),
                   jax.ShapeDtypeStruct((2 * PEER_HEADS, PEER_NKEYS, T), F32)],
        compiler_params=_cparams(("parallel",)),
    )(orw, bonus, g, oret, p_ret, p_gate, x2, lng, lnb, rlng, bd, pa, pb, wo, n2, wq, sk)


def _top16(s):
    rows16 = lax.broadcasted_iota(jnp.int32, (PEER_TOPK, s.shape[1]), 0)
    vals = jnp.zeros((PEER_TOPK, s.shape[1]), F32)
    for a in range(PEER_TOPK):
        m = jnp.max(s, axis=0, keepdims=True)
        vals = jnp.where(rows16 == a, m, vals)
        s = jnp.where(s == m, NEG_BIG, s)
    return vals


def _topk_prep_kernel(sc_ref, c_ref, e1_ref, rk_ref, e2_ref):
    s1 = sc_ref[0]
    s2 = sc_ref[1]
    v1 = _top16(s1)
    v2 = _top16(s2)
    tl = s1.shape[1]
    cands = [v1[0:1, :] + v2]
    for a in range(1, 8):
        cands.append(v1[a:a + 1, :] + v2[0:8, :])
    cands.append(v1[8:16, :] + v2[0:1, :])
    tau = _top16(jnp.concatenate(cands, axis=0))[PEER_TOPK - 1:PEER_TOPK, :]

    cut = jnp.zeros_like(s1)
    rank2 = jnp.zeros_like(s2)
    cut16 = jnp.zeros_like(v1)
    for b in range(PEER_TOPK):
        vb = v2[b:b + 1, :]
        cut = cut + jnp.where(s1 + vb >= tau, 1.0, 0.0)
        rank2 = rank2 + jnp.where(vb > s2, 1.0, 0.0)
        cut16 = cut16 + jnp.where(v1 + vb >= tau, 1.0, 0.0)
    e1v = jnp.exp(v1 - v1[0:1, :])
    e2v = jnp.exp(v2 - v2[0:1, :])
    zacc = jnp.zeros_like(v1)
    for b in range(PEER_TOPK):
        zacc = zacc + jnp.where(cut16 > b, e1v, 0.0) * e2v[b:b + 1, :]
    z = jnp.sum(zacc, axis=0, keepdims=True)
    c_ref[0] = cut
    e1_ref[0] = jnp.exp(s1 - v1[0:1, :]) * (0.5 / z)
    rk_ref[0] = rank2.astype(BF16)
    e2_ref[0] = jnp.exp(s2 - v2[0:1, :]).astype(BF16)
    del tl


def _topk_prep(sc):
    _, nk, T = sc.shape
    tl = TOPK_TILE
    blk = pl.BlockSpec((1, nk, tl), lambda h, t: (h, 0, t))
    sf = jax.ShapeDtypeStruct((PEER_HEADS, nk, T), F32)
    sb = jax.ShapeDtypeStruct((PEER_HEADS, nk, T), BF16)
    return pl.pallas_call(
        _topk_prep_kernel,
        name="topk_prep",
        grid=(PEER_HEADS, T // tl),
        in_specs=[pl.BlockSpec((2, nk, tl), lambda h, t: (h, 0, t))],
        out_specs=[blk, blk, blk, blk],
        out_shape=[sf, sf, sb, sb],
        compiler_params=_cparams(("parallel", "parallel")),
    )(sc)


def _gelu2(x):
    return x * (1.0 + lax.erf(x * (2.0 ** -0.5)))


def _peer_kernel(h_ref, u_ref, vt_ref, c_ref, e1_ref, rk_ref, e2_ref, x1_ref, nf_ref, y_ref,
                 acc_ref, act_ref, p_ref):
    e = pl.program_id(1)
    ne = pl.num_programs(1)
    nrow = PEER_EXP_TILE // PEER_NKEYS

    @pl.when(e == 0)
    def _():
        acc_ref[...] = jnp.zeros_like(acc_ref)

    act_ref[...] = _dot_nt(u_ref[...], h_ref[...])
    base = pl.multiple_of(e * nrow, nrow)
    cuts = [c_ref[h, pl.ds(base, nrow), :] for h in range(PEER_HEADS)]
    e1s = [e1_ref[h, pl.ds(base, nrow), :] for h in range(PEER_HEADS)]
    for il in range(nrow):
        rows = slice(il * PEER_NKEYS, (il + 1) * PEER_NKEYS)
        w = None
        for h in range(PEER_HEADS):
            cut = cuts[h][il:il + 1, :].astype(BF16)
            e1 = e1s[h][il:il + 1, :].astype(BF16)
            term = jnp.where(rk_ref[h] < cut, e2_ref[h], jnp.zeros((), BF16)) * e1
            w = term if w is None else w + term
        p_ref[rows, :] = w * _gelu2(act_ref[rows, :]).astype(BF16)
    acc_ref[...] += _dot(vt_ref[...], p_ref[...])

    @pl.when(e == ne - 1)
    def _():
        y = x1_ref[...] + acc_ref[...].T
        y_ref[...] = _rmsnorm(y, nf_ref[...])


def _peer(h2, u_bf16, vt_bf16, cut, e1, rank2, e2, x1, nf):
    T = h2.shape[0]
    tT = PEER_TOK_TILE
    eT = PEER_EXP_TILE
    sel = pl.BlockSpec((PEER_HEADS, PEER_NKEYS, tT), lambda t, e: (0, 0, t))
    return pl.pallas_call(
        _peer_kernel,
        name="peer",
        grid=(T // tT, PEER_EXPERTS // eT),
        in_specs=[pl.BlockSpec((tT, D_MODEL), lambda t, e: (t, 0)),
                  pl.BlockSpec((eT, D_MODEL), lambda t, e: (e, 0)),
                  pl.BlockSpec((D_MODEL, eT), lambda t, e: (0, e)),
                  sel, sel, sel, sel,
                  pl.BlockSpec((tT, D_MODEL), lambda t, e: (t, 0)),
                  pl.BlockSpec((1, D_MODEL), lambda t, e: (0, 0))],
        out_specs=pl.BlockSpec((tT, D_MODEL), lambda t, e: (t, 0)),
        out_shape=jax.ShapeDtypeStruct((T, D_MODEL), F32),
        scratch_shapes=[pltpu.VMEM((D_MODEL, tT), F32),
                        pltpu.VMEM((eT, tT), F32),
                        pltpu.VMEM((eT, tT), BF16)],
        compiler_params=_cparams(("parallel", "arbitrary")),
    )(h2, u_bf16, vt_bf16, cut, e1, rank2, e2, x1, nf)


def _rope_tables(S):
    half = HEAD_DIM // 2
    inv = ROPE_BASE ** (-jnp.arange(half, dtype=F32) / half)
    ang = jnp.arange(S, dtype=F32)[:, None] * inv[None, :]
    c = jnp.cos(ang)
    s = jnp.sin(ang)
    reps = LANES // HEAD_DIM
    cos_t = jnp.tile(jnp.concatenate([c, c], axis=1), (1, reps))
    sin_t = jnp.tile(jnp.concatenate([-s, s], axis=1), (1, reps))
    return cos_t, sin_t


def _trunk(x, w):
    B, S, _ = x.shape
    T = B * S
    x2 = x.reshape(T, D_MODEL)
    p_rwkv, p_ret, p_gate = _inproj(x2, w["norm1_g"], w["w_in"])
    r, v, kap, lw, kd, b, bonus, g = _rwkv_prep(
        p_rwkv.reshape(B, S, RWKV_COLS), w["mu"], w["w0"], w["w2"], w["a0"], w["a2"], w["g2"],
        w["k_k"], w["k_a"], w["r_k"], w["bd"])
    orw = _rwkv_scan(r, v, kap, lw, kd, b)
    cos_t, sin_t = _rope_tables(S)
    oret = _retention(p_ret.reshape(B, S, RET_COLS), cos_t, sin_t, w["lg"])
    x1, h2, sc = _merge(orw.reshape(2, T, WIDTH), bonus.reshape(T, WIDTH), g.reshape(T, WIDTH),
                        oret.reshape(2, T, WIDTH), p_ret, p_gate, x2,
                        w["ln_g"], w["ln_b"], w["ret_ln_g"], w["bd"], w["proj_a"], w["proj_b"], w["w_out"],
                        w["norm2_g"], w["wq"], w["sk"])
    cut, e1, rank2, e2 = _topk_prep(sc)
    y = _peer(h2, w["u"], w["vt"], cut, e1, rank2, e2, x1, w["normf_g"])
    return y.reshape(B, S, D_MODEL)


def kernel(x_prompt, x_sample, norm1_g, w_in, rwkv_mu, rwkv_w0, rwkv_w2, rwkv_a0, rwkv_a2, rwkv_g2, rwkv_k_k,
           rwkv_k_a, rwkv_r_k, rwkv_ln_g, rwkv_ln_b, ret_ln_g, proj_a, proj_b, w_out, norm2_g, peer_wq,
           peer_subkeys, peer_u, peer_v, normf_g):
    assert norm1_g.shape[0] == 1, "single-layer trunk"
    head_id = jnp.arange(WIDTH) // HEAD_DIM
    w = {
        "norm1_g": norm1_g[0][None, :],
        "w_in": w_in[0].astype(BF16),
        "mu": rwkv_mu[0],
        "w0": rwkv_w0[0],
        "w2": rwkv_w2[0].astype(BF16),
        "a0": rwkv_a0[0],
        "a2": rwkv_a2[0].astype(BF16),
        "g2": rwkv_g2[0].astype(BF16),
        "k_k": rwkv_k_k[0][None, :],
        "k_a": rwkv_k_a[0][None, :],
        "r_k": rwkv_r_k[0].reshape(1, WIDTH),
        "ln_g": rwkv_ln_g[0][None, :],
        "ln_b": rwkv_ln_b[0][None, :],
        "ret_ln_g": ret_ln_g[0][None, :],
        "bd": (head_id[:, None] == head_id[None, :]).astype(BF16),
        "lg": jnp.broadcast_to(jnp.log1p(-jnp.exp2(-5.0 - jnp.arange(N_HEADS, dtype=F32)))[:, None],
                               (N_HEADS, LANES)),
        "proj_a": proj_a[0].astype(BF16),
        "proj_b": proj_b[0].astype(BF16),
        "w_out": w_out[0].astype(BF16),
        "norm2_g": norm2_g[0][None, :],
        "wq": peer_wq[0].astype(BF16),
        "sk": peer_subkeys[0].reshape(2 * PEER_HEADS, PEER_NKEYS, 128).astype(BF16),
        "u": peer_u[0].astype(BF16),
        "vt": peer_v[0].T.astype(BF16),
        "normf_g": normf_g[None, :],
    }
    return (_trunk(x_prompt, w), _trunk(x_sample, w))
```

```python
import functools

import jax
import jax.numpy as jnp
from jax import lax
from jax.experimental import pallas as pl
from jax.experimental.pallas import tpu as pltpu

F32 = jnp.float32
BF16 = jnp.bfloat16

D_MODEL = 1024
HEAD_DIM = 64
N_HEADS = 8
WIDTH = N_HEADS * HEAD_DIM
RWKV_COLS = 1920
RET_COLS = 2048
GATE_COLS = 2048
RWKV_GN_EPS = 64e-5
RET_GN_EPS = 1e-6
NORM_EPS = 1e-6
ROPE_BASE = 10000.0
PEER_HEADS = 8
PEER_NKEYS = 128
PEER_TOPK = 16
PEER_EXPERTS = PEER_NKEYS * PEER_NKEYS

LANES = 128
CHUNK = 64
SEQ_TILE = 256
SCAN_GROUPS = 4
TOK_TILE = 256
PEER_TOK_TILE = 512
PEER_EXP_TILE = 1024
TOPK_TILE = 512
VMEM_LIMIT = 56 * 1024 * 1024
NEG_BIG = -1e30

_NT = (((1,), (1,)), ((), ()))


def _cparams(sem):
    return pltpu.CompilerParams(dimension_semantics=sem, vmem_limit_bytes=VMEM_LIMIT)


def _dot(a, b):
    return jnp.dot(a, b, preferred_element_type=F32)


def _dot_nt(a, b):
    return lax.dot_general(a, b, _NT, preferred_element_type=F32)


def _dot_tn(a, b):
    return lax.dot_general(a, b, (((0,), (0,)), ((), ())), preferred_element_type=F32)


def _split2(x):
    hi = x.astype(BF16)
    lo = (x - hi.astype(F32)).astype(BF16)
    return hi, lo


def _ones_mm(m_bf16, x):
    x0 = x.astype(BF16)
    r1 = x - x0.astype(F32)
    x1 = r1.astype(BF16)
    x2 = (r1 - x1.astype(F32)).astype(BF16)
    return _dot(m_bf16, x0) + (_dot(m_bf16, x1) + _dot(m_bf16, x2))


def _segsum(x, bd):
    hi, lo = _split2(x)
    return _dot(hi, bd) + _dot(lo, bd)


def _rmsnorm(x, g):
    return x * lax.rsqrt(jnp.mean(x * x, axis=-1, keepdims=True) + NORM_EPS) * g


def _inproj_kernel(x_ref, g_ref, w_ref, o1_ref, o2_ref, o3_ref):
    hb = _rmsnorm(x_ref[...], g_ref[...]).astype(BF16)
    o1_ref[...] = _dot(hb, w_ref[:, 0:RWKV_COLS])
    o2_ref[...] = _dot(hb, w_ref[:, RWKV_COLS:RWKV_COLS + RET_COLS])
    o3_ref[...] = _dot(hb, w_ref[:, RWKV_COLS + RET_COLS:])


def _inproj(x2, g, w_bf16):
    T = x2.shape[0]
    tm = TOK_TILE
    ncol = w_bf16.shape[1]
    return pl.pallas_call(
        _inproj_kernel,
        name="inproj",
        grid=(T // tm,),
        in_specs=[
            pl.BlockSpec((tm, D_MODEL), lambda i: (i, 0)),
            pl.BlockSpec((1, D_MODEL), lambda i: (0, 0)),
            pl.BlockSpec((D_MODEL, ncol), lambda i: (0, 0)),
        ],
        out_specs=[
            pl.BlockSpec((tm, RWKV_COLS), lambda i: (i, 0)),
            pl.BlockSpec((tm, RET_COLS), lambda i: (i, 0)),
            pl.BlockSpec((tm, GATE_COLS), lambda i: (i, 0)),
        ],
        out_shape=[
            jax.ShapeDtypeStruct((T, RWKV_COLS), F32),
            jax.ShapeDtypeStruct((T, RET_COLS), F32),
            jax.ShapeDtypeStruct((T, GATE_COLS), F32),
        ],
        compiler_params=_cparams(("parallel",)),
    )(x2, g, w_bf16)


def _softplus(y):
    return jnp.maximum(y, 0.0) + jnp.log1p(jnp.exp(-jnp.abs(y)))


def _rwkv_prep_kernel(p_ref, pprev_ref, pnext_ref, mu_ref, w0_ref, w2_ref, a0_ref, a2_ref, g2_ref,
                      kk_ref, ka_ref, rk_ref, bd_ref,
                      r_ref, v_ref, kap_ref, lw_ref, kd_ref, b_ref, bonus_ref, g_ref):
    i = pl.program_id(1)
    n = pl.num_programs(1)
    p = p_ref[0]
    tm = p.shape[0]
    prev_row = jnp.where(i > 0, pprev_ref[0, 7:8, :], 0.0)
    next_row = jnp.where(i < n - 1, pnext_ref[0, 0:1, :], 0.0)
    row = lax.broadcasted_iota(jnp.int32, p.shape, 0)
    prev = jnp.where(row == 0, prev_row, pltpu.roll(p, 1, 0))
    nxt = jnp.where(row == tm - 1, next_row, pltpu.roll(p, tm - 1, 0))
    ps = p + mu_ref[0:1, :] * (prev - p) + mu_ref[1:2, :] * (nxt - p)

    bd = bd_ref[...]
    r = ps[:, 0:WIDTH]
    k = ps[:, WIDTH:2 * WIDTH]
    v = ps[:, 2 * WIDTH:3 * WIDTH]
    kk = k * kk_ref[...]
    kappa = kk / jnp.maximum(jnp.sqrt(_segsum(kk * kk, bd)), 1e-12)
    r_ref[0] = r
    v_ref[0] = v
    kap_ref[0] = kappa
    ksum = jnp.zeros_like(k)
    for d in range(2):
        wl = ps[:, 1536 + 64 * d:1600 + 64 * d]
        al = ps[:, 1664 + 64 * d:1728 + 64 * d]
        z = w0_ref[d:d + 1, :] + _dot(jnp.tanh(wl).astype(BF16), w2_ref[d])
        w_log = -_softplus(-z) - 0.5
        lw_ref[d, 0] = -jnp.exp(w_log)
        a = jax.nn.sigmoid(a0_ref[d:d + 1, :] + _dot(al.astype(BF16), a2_ref[d]))
        kd = k * (1.0 + (a - 1.0) * ka_ref[...])
        kd_ref[d, 0] = kd
        b_ref[d, 0] = kappa * a
        ksum = ksum + kd
    bonus_ref[0] = _segsum(r * (0.5 * ksum) * rk_ref[...], bd) * v
    g_ref[0] = _dot(jax.nn.sigmoid(ps[:, 1792:1920]).astype(BF16), g2_ref[...])


def _rwkv_prep(p_rwkv, mu, w0, w2, a0, a2, g2, k_k, k_a, r_k, bd):
    B, S, _ = p_rwkv.shape
    tm = SEQ_TILE
    n = S // tm
    nb8 = S // 8
    full = lambda shape: pl.BlockSpec(shape, lambda b, i: (0,) * len(shape))
    tok = pl.BlockSpec((1, tm, WIDTH), lambda b, i: (b, i, 0))
    tok2 = pl.BlockSpec((2, 1, tm, WIDTH), lambda b, i: (0, b, i, 0))
    s1 = jax.ShapeDtypeStruct((B, S, WIDTH), F32)
    s2 = jax.ShapeDtypeStruct((2, B, S, WIDTH), F32)
    return pl.pallas_call(
        _rwkv_prep_kernel,
        name="rwkv_prep",
        grid=(B, n),
        in_specs=[
            pl.BlockSpec((1, tm, RWKV_COLS), lambda b, i: (b, i, 0)),
            pl.BlockSpec((1, 8, RWKV_COLS), lambda b, i: (b, jnp.maximum(i * (tm // 8) - 1, 0), 0)),
            pl.BlockSpec((1, 8, RWKV_COLS), lambda b, i: (b, jnp.minimum((i + 1) * (tm // 8), nb8 - 1), 0)),
            full((2, RWKV_COLS)), full((2, WIDTH)), full((2, 64, WIDTH)), full((2, WIDTH)),
            full((2, 64, WIDTH)), full((128, WIDTH)), full((1, WIDTH)), full((1, WIDTH)), full((1, WIDTH)),
            full((WIDTH, WIDTH)),
        ],
        out_specs=[tok, tok, tok, tok2, tok2, tok2, tok, tok],
        out_shape=[s1, s1, s1, s2, s2, s2, s1, s1],
        compiler_params=_cparams(("parallel", "parallel")),
    )(p_rwkv, p_rwkv, p_rwkv, mu, w0, w2, a0, a2, g2, k_k, k_a, r_k, bd)


def _scan_groups(groups, states, fwd, strict, incl, cum_b):
    tm = SEQ_TILE
    nck = tm // CHUNK
    ng = len(groups)
    lane = lax.broadcasted_iota(jnp.int32, (1, LANES), 1)
    gs, RRb, Ath, Rth, Vhb = [], [], [], [], []
    for (R, V, KAP, LW, KD, BV) in groups:
        g = _ones_mm(cum_b, LW)
        eng = jnp.exp(-g)
        At = -KAP * jnp.exp(g - LW)
        Rt = R * jnp.exp(g)
        gs.append(g)
        RRb.append(jnp.concatenate([BV * eng, KD * eng], axis=0).astype(BF16))
        for h in range(2):
            mh = jnp.where((lane // HEAD_DIM) == h, 1.0, 0.0)
            Ath.append(At * mh)
            Rth.append(Rt * mh)
            Vhb.append((V * mh).astype(BF16))
    chains = range(2 * ng)
    Q = [_dot_nt(jnp.concatenate([Ath[c], Rth[c]], axis=0).astype(BF16), RRb[c // 2]) for c in chains]
    Nb = [jnp.where(strict, Q[c][:tm, :tm], 0.0).astype(BF16) for c in chains]
    Aak = [jnp.where(strict, Q[c][:tm, tm:], 0.0).astype(BF16) for c in chains]
    Arb = [jnp.where(incl, Q[c][tm:, :tm], 0.0).astype(BF16) for c in chains]
    Ark = [jnp.where(incl, Q[c][tm:, tm:], 0.0).astype(BF16) for c in chains]
    X = [jnp.concatenate([Ath[c], _dot(Aak[c], Vhb[c])], axis=1) for c in chains]
    X = [X[c] + _dot(Nb[c], X[c].astype(BF16)) for c in chains]
    Pb = Nb
    for _ in range(5):
        Pb = [_dot(Pb[c], Pb[c]).astype(BF16) for c in chains]
        X = [X[c] + _dot(Pb[c], X[c].astype(BF16)) for c in chains]
    Rhh = [Rth[c] + _dot(Arb[c], X[c][:, :LANES].astype(BF16)) for c in chains]
    Ohh = [_dot(Arb[c], X[c][:, LANES:].astype(BF16)) + _dot(Ark[c], Vhb[c]) for c in chains]

    r128 = lax.broadcasted_iota(jnp.int32, (LANES, LANES), 0)
    c128 = lax.broadcasted_iota(jnp.int32, (LANES, LANES), 1)
    bdm = (r128 // HEAD_DIM) == (c128 // HEAD_DIM)
    eye = r128 == c128
    colc = lax.broadcasted_iota(jnp.int32, (1, tm), 1) // CHUNK
    Rh, Oh, Gs, Hs = [], [], [], []
    for gi, (R, V, KAP, LW, KD, BV) in enumerate(groups):
        g = gs[gi]
        Wtb = (X[2 * gi][:, :LANES] + X[2 * gi + 1][:, :LANES]).astype(BF16)
        Utb = (X[2 * gi][:, LANES:] + X[2 * gi + 1][:, LANES:]).astype(BF16)
        Rh.append(Rhh[2 * gi] + Rhh[2 * gi + 1])
        Oh.append(Ohh[2 * gi] + Ohh[2 * gi + 1])
        ends = []
        for c in range(nck):
            e_f = g[c * CHUNK + CHUNK - 1:c * CHUNK + CHUNK, :]
            e_b = g[c * CHUNK:c * CHUNK + 1, :]
            ends.append(jnp.where(fwd, e_f, e_b))
        gend = jnp.concatenate([jnp.broadcast_to(e, (CHUNK, LANES)) for e in ends], axis=0)
        egc = jnp.exp(gend - g)
        BhT = (BV * egc).T.astype(BF16)
        KhT = (KD * egc).T.astype(BF16)
        Vb = V.astype(BF16)
        Gg, Hg = [], []
        for c in range(nck):
            cm = colc == c
            Bc = jnp.where(cm, BhT, jnp.zeros((), BF16))
            Kc = jnp.where(cm, KhT, jnp.zeros((), BF16))
            Gg.append(jnp.where(eye, jnp.exp(ends[c]), 0.0) + jnp.where(bdm, _dot(Bc, Wtb), 0.0))
            Hg.append(jnp.where(bdm, _dot(Bc, Utb) + _dot(Kc, Vb), 0.0))
        Gs.append(Gg)
        Hs.append(Hg)

    St = list(states)
    outs = [[None] * nck for _ in range(ng)]
    for step in range(nck):
        c_f = step
        c_b = nck - 1 - step
        rows_f = slice(c_f * CHUNK, (c_f + 1) * CHUNK)
        rows_b = slice(c_b * CHUNK, (c_b + 1) * CHUNK)
        for gi in range(ng):
            Rc = jnp.where(fwd, Rh[gi][rows_f], Rh[gi][rows_b])
            Oc = jnp.where(fwd, Oh[gi][rows_f], Oh[gi][rows_b])
            Gc = jnp.where(fwd, Gs[gi][c_f], Gs[gi][c_b])
            Hc = jnp.where(fwd, Hs[gi][c_f], Hs[gi][c_b])
            Stb = St[gi].astype(BF16)
            outs[gi][step] = _dot(Rc.astype(BF16), Stb) + Oc
            St[gi] = _dot(Gc.astype(BF16), Stb) + Hc
    res = []
    for gi in range(ng):
        o_f = jnp.concatenate(outs[gi], axis=0)
        o_b = jnp.concatenate(outs[gi][::-1], axis=0)
        res.append(jnp.where(fwd, o_f, o_b))
    return res, St


def _rwkv_scan_kernel(r_ref, v_ref, kap_ref, lw_ref, kd_ref, b_ref, o_ref, st_ref):
    d = pl.program_id(2)
    i = pl.program_id(3)
    tm = SEQ_TILE

    @pl.when(i == 0)
    def _():
        st_ref[...] = jnp.zeros_like(st_ref)

    fwd = d == 0
    ri = lax.broadcasted_iota(jnp.int32, (tm, tm), 0)
    ci = lax.broadcasted_iota(jnp.int32, (tm, tm), 1)
    same = (ri // CHUNK) == (ci // CHUNK)
    ahead = jnp.where(fwd, ri - ci, ci - ri)
    strict = same & (ahead > 0)
    incl = same & (ahead >= 0)
    cum_b = jnp.where(incl, 1.0, 0.0).astype(BF16)
    lanes = [slice(grp * LANES, (grp + 1) * LANES) for grp in range(SCAN_GROUPS)]
    groups = [(r_ref[0, :, ls], v_ref[0, :, ls], kap_ref[0, :, ls], lw_ref[0, 0, :, ls],
               kd_ref[0, 0, :, ls], b_ref[0, 0, :, ls]) for ls in lanes]
    outs, sts = _scan_groups(groups, [st_ref[grp] for grp in range(SCAN_GROUPS)], fwd, strict, incl, cum_b)
    for grp, ls in enumerate(lanes):
        o_ref[0, 0, :, ls] = outs[grp]
        st_ref[grp] = sts[grp]


def _rwkv_scan(r, v, kap, lw, kd, b):
    B, S, _ = r.shape
    tm = SEQ_TILE
    n = S // tm
    wb = SCAN_GROUPS * LANES
    pos = lambda d, i: jnp.where(d == 0, i, n - 1 - i)
    tok = pl.BlockSpec((1, tm, wb), lambda bb, hp, d, i: (bb, pos(d, i), hp))
    tok2 = pl.BlockSpec((1, 1, tm, wb), lambda bb, hp, d, i: (d, bb, pos(d, i), hp))
    return pl.pallas_call(
        _rwkv_scan_kernel,
        name="rwkv_scan",
        grid=(B, WIDTH // wb, 2, n),
        in_specs=[tok, tok, tok, tok2, tok2, tok2],
        out_specs=tok2,
        out_shape=jax.ShapeDtypeStruct((2, B, S, WIDTH), F32),
        scratch_shapes=[pltpu.VMEM((SCAN_GROUPS, LANES, LANES), F32)],
        compiler_params=_cparams(("parallel", "parallel", "parallel", "arbitrary")),
    )(r, v, kap, lw, kd, b)


def _retention_kernel(q_ref, k_ref, v_ref, cos_ref, sin_ref, lg_ref, o_ref, st_ref):
    hp = pl.program_id(1)
    d = pl.program_id(2)
    i = pl.program_id(3)
    tm = SEQ_TILE

    @pl.when(i == 0)
    def _():
        st_ref[...] = jnp.zeros_like(st_ref)

    fwd = d == 0
    lane = lax.broadcasted_iota(jnp.int32, (1, LANES), 1)
    lg0 = lg_ref[pl.ds(2 * hp, 1), :]
    lg1 = lg_ref[pl.ds(2 * hp + 1, 1), :]
    lgl = jnp.where(lane < HEAD_DIM, lg0, lg1)

    cosv = cos_ref[...]
    sinv = sin_ref[...]
    half = HEAD_DIM // 2
    first = (lane % HEAD_DIM) < half

    def rot(t):
        sw = jnp.where(first, pltpu.roll(t, LANES - half, 1), pltpu.roll(t, half, 1))
        return t * cosv + sw * sinv

    q = rot(q_ref[0]) * (HEAD_DIM ** -0.5)
    k = rot(k_ref[0])
    v = v_ref[0]

    rowf = lax.broadcasted_iota(jnp.int32, (tm, LANES), 0).astype(F32)
    e_q = jnp.where(fwd, rowf + 1.0, tm - rowf)
    e_k = jnp.where(fwd, tm - 1.0 - rowf, rowf)
    St = st_ref[...]
    cross = _dot((q * jnp.exp(e_q * lgl)).astype(BF16), St.astype(BF16))

    kz = (k * jnp.exp(e_k * lgl)).astype(BF16)
    r128 = lax.broadcasted_iota(jnp.int32, (LANES, LANES), 0)
    c128 = lax.broadcasted_iota(jnp.int32, (LANES, LANES), 1)
    bdm = (r128 // HEAD_DIM) == (c128 // HEAD_DIM)
    kv = _dot_tn(kz, v.astype(BF16))
    st_ref[...] = jnp.exp(tm * lgl) * St + jnp.where(bdm, kv, 0.0)

    @pl.when(fwd)
    def _():
        ri = lax.broadcasted_iota(jnp.int32, (tm, tm), 0)
        ci = lax.broadcasted_iota(jnp.int32, (tm, tm), 1)
        dist = jnp.abs(ri - ci).astype(F32)
        acc = cross
        kb = k.astype(BF16)
        for h in range(2):
            mh = jnp.where((lane // HEAD_DIM) == h, 1.0, 0.0)
            lgh = lg0 if h == 0 else lg1
            s = _dot_nt((q * mh).astype(BF16), kb)
            s = s * jnp.exp(dist * lgh[:, 0:1])
            acc = acc + _dot(s.astype(BF16), (v * mh).astype(BF16))
        o_ref[0, 0] = acc

    @pl.when(jnp.logical_not(fwd))
    def _():
        o_ref[0, 0] = cross


def _retention(p_ret, cos_t, sin_t, lg):
    B, S, _ = p_ret.shape
    tm = SEQ_TILE
    n = S // tm
    nhp = WIDTH // LANES
    pos = lambda d, i: jnp.where(d == 0, i, n - 1 - i)
    col = lambda off: pl.BlockSpec((1, tm, LANES), lambda bb, hp, d, i: (bb, pos(d, i), off + hp))
    tab = pl.BlockSpec((tm, LANES), lambda bb, hp, d, i: (pos(d, i), 0))
    return pl.pallas_call(
        _retention_kernel,
        name="retention",
        grid=(B, nhp, 2, n),
        in_specs=[col(0), col(nhp), col(2 * nhp), tab, tab,
                  pl.BlockSpec((N_HEADS, LANES), lambda bb, hp, d, i: (0, 0))],
        out_specs=pl.BlockSpec((1, 1, tm, LANES), lambda bb, hp, d, i: (d, bb, pos(d, i), hp)),
        out_shape=jax.ShapeDtypeStruct((2, B, S, WIDTH), F32),
        scratch_shapes=[pltpu.VMEM((LANES, LANES), F32)],
        compiler_params=_cparams(("parallel", "parallel", "parallel", "arbitrary")),
    )(p_ret, p_ret, p_ret, cos_t, sin_t, lg)


def _head_norm(o, bd, eps):
    mu = _segsum(o, bd) * (1.0 / HEAD_DIM)
    dlt = o - mu
    var = _segsum(dlt * dlt, bd) * (1.0 / HEAD_DIM)
    return dlt * lax.rsqrt(var + eps)


def _merge_kernel(orw_ref, bonus_ref, g_ref, oret_ref, qg_ref, gate_ref, x_ref,
                  lng_ref, lnb_ref, rlng_ref, bd_ref, pa_ref, pb_ref, wo_ref, n2_ref, wq_ref, sk_ref,
                  x1_ref, h2_ref, sc_ref):
    bd = bd_ref[...]
    o = _head_norm(orw_ref[0] + orw_ref[1], bd, RWKV_GN_EPS) * lng_ref[...] + lnb_ref[...]
    o_a = (o + bonus_ref[...]) * g_ref[...]
    o_b = _head_norm(oret_ref[0] + oret_ref[1], bd, RET_GN_EPS) * rlng_ref[...] * jax.nn.silu(qg_ref[...])
    y_a = _dot(o_a.astype(BF16), pa_ref[...])
    y_b = _dot(o_b.astype(BF16), pb_ref[...])
    gate = gate_ref[...]
    mixed = jax.nn.sigmoid(gate[:, :D_MODEL]) * y_a + jax.nn.sigmoid(gate[:, D_MODEL:]) * y_b
    x1 = x_ref[...] + _dot(mixed.astype(BF16), wo_ref[...])
    x1_ref[...] = x1
    h2 = _rmsnorm(x1, n2_ref[...])
    hb = h2.astype(BF16)
    h2_ref[...] = hb
    q = _dot(hb, wq_ref[...]).astype(BF16)
    for j in range(2 * PEER_HEADS):
        sc_ref[j] = _dot_nt(sk_ref[j], q[:, j * 128:(j + 1) * 128])


def _merge(orw, bonus, g, oret, p_ret, p_gate, x2, lng, lnb, rlng, bd, pa, pb, wo, n2, wq, sk):
    T = x2.shape[0]
    tm = TOK_TILE
    full = lambda shape: pl.BlockSpec(shape, lambda i: (0,) * len(shape))
    tok = lambda w: pl.BlockSpec((tm, w), lambda i: (i, 0))
    two = pl.BlockSpec((2, tm, WIDTH), lambda i: (0, i, 0))
    return pl.pallas_call(
        _merge_kernel,
        name="merge",
        grid=(T // tm,),
        in_specs=[two, tok(WIDTH), tok(WIDTH), two,
                  pl.BlockSpec((tm, WIDTH), lambda i: (i, 3)),
                  tok(GATE_COLS), tok(D_MODEL),
                  full((1, WIDTH)), full((1, WIDTH)), full((1, WIDTH)), full((WIDTH, WIDTH)),
                  full((WIDTH, D_MODEL)), full((WIDTH, D_MODEL)), full((D_MODEL, D_MODEL)), full((1, D_MODEL)),
                  full((D_MODEL, 2 * PEER_HEADS * 128)), full((2 * PEER_HEADS, 128, 128))],
        out_specs=[tok(D_MODEL), tok(D_MODEL),
                   pl.BlockSpec((2 * PEER_HEADS, PEER_NKEYS, tm), lambda i: (0, 0, i))],
        out_shape=[jax.ShapeDtypeStruct((T, D_MODEL), F32),
                   jax.ShapeDtypeStruct((T, D_MODEL), BF16),
                   jax.ShapeDtypeStruct((2 * PEER_HEADS, PEER_NKEYS, T), F32)],
        compiler_params=_cparams(("parallel",)),
    )(orw, bonus, g, oret, p_ret, p_gate, x2, lng, lnb, rlng, bd, pa, pb, wo, n2, wq, sk)


def _top16(s):
    rows16 = lax.broadcasted_iota(jnp.int32, (PEER_TOPK, s.shape[1]), 0)
    vals = jnp.zeros((PEER_TOPK, s.shape[1]), F32)
    for a in range(PEER_TOPK):
        m = jnp.max(s, axis=0, keepdims=True)
        vals = jnp.where(rows16 == a, m, vals)
        s = jnp.where(s == m, NEG_BIG, s)
    return vals


def _topk_prep_kernel(sc_ref, c_ref, e1_ref, rk_ref, e2_ref):
    s1 = sc_ref[0]
    s2 = sc_ref[1]
    v1 = _top16(s1)
    v2 = _top16(s2)
    tl = s1.shape[1]
    cands = [v1[0:1, :] + v2]
    for a in range(1, 8):
        cands.append(v1[a:a + 1, :] + v2[0:8, :])
    cands.append(v1[8:16, :] + v2[0:1, :])
    tau = _top16(jnp.concatenate(cands, axis=0))[PEER_TOPK - 1:PEER_TOPK, :]

    cut = jnp.zeros_like(s1)
    rank2 = jnp.zeros_like(s2)
    cut16 = jnp.zeros_like(v1)
    for b in range(PEER_TOPK):
        vb = v2[b:b + 1, :]
        cut = cut + jnp.where(s1 + vb >= tau, 1.0, 0.0)
        rank2 = rank2 + jnp.where(vb > s2, 1.0, 0.0)
        cut16 = cut16 + jnp.where(v1 + vb >= tau, 1.0, 0.0)
    e1v = jnp.exp(v1 - v1[0:1, :])
    e2v = jnp.exp(v2 - v2[0:1, :])
    zacc = jnp.zeros_like(v1)
    for b in range(PEER_TOPK):
        zacc = zacc + jnp.where(cut16 > b, e1v, 0.0) * e2v[b:b + 1, :]
    z = jnp.sum(zacc, axis=0, keepdims=True)
    c_ref[0] = cut
    e1_ref[0] = jnp.exp(s1 - v1[0:1, :]) * (0.5 / z)
    rk_ref[0] = rank2.astype(BF16)
    e2_ref[0] = jnp.exp(s2 - v2[0:1, :]).astype(BF16)
    del tl


def _topk_prep(sc):
    _, nk, T = sc.shape
    tl = TOPK_TILE
    blk = pl.BlockSpec((1, nk, tl), lambda h, t: (h, 0, t))
    sf = jax.ShapeDtypeStruct((PEER_HEADS, nk, T), F32)
    sb = jax.ShapeDtypeStruct((PEER_HEADS, nk, T), BF16)
    return pl.pallas_call(
        _topk_prep_kernel,
        name="topk_prep",
        grid=(PEER_HEADS, T // tl),
        in_specs=[pl.BlockSpec((2, nk, tl), lambda h, t: (h, 0, t))],
        out_specs=[blk, blk, blk, blk],
        out_shape=[sf, sf, sb, sb],
        compiler_params=_cparams(("parallel", "parallel")),
    )(sc)


def _gelu2(x):
    return x * (1.0 + lax.erf(x * (2.0 ** -0.5)))


def _peer_kernel(h_ref, u_ref, vt_ref, c_ref, e1_ref, rk_ref, e2_ref, x1_ref, nf_ref, y_ref,
                 acc_ref, act_ref, p_ref):
    e = pl.program_id(1)
    ne = pl.num_programs(1)
    nrow = PEER_EXP_TILE // PEER_NKEYS

    @pl.when(e == 0)
    def _():
        acc_ref[...] = jnp.zeros_like(acc_ref)

    act_ref[...] = _dot_nt(u_ref[...], h_ref[...])
    base = pl.multiple_of(e * nrow, nrow)
    cuts = [c_ref[h, pl.ds(base, nrow), :] for h in range(PEER_HEADS)]
    e1s = [e1_ref[h, pl.ds(base, nrow), :] for h in range(PEER_HEADS)]
    for il in range(nrow):
        rows = slice(il * PEER_NKEYS, (il + 1) * PEER_NKEYS)
        w = None
        for h in range(PEER_HEADS):
            cut = cuts[h][il:il + 1, :].astype(BF16)
            e1 = e1s[h][il:il + 1, :].astype(BF16)
            term = jnp.where(rk_ref[h] < cut, e2_ref[h], jnp.zeros((), BF16)) * e1
            w = term if w is None else w + term
        p_ref[rows, :] = w * _gelu2(act_ref[rows, :]).astype(BF16)
    acc_ref[...] += _dot(vt_ref[...], p_ref[...])

    @pl.when(e == ne - 1)
    def _():
        y = x1_ref[...] + acc_ref[...].T
        y_ref[...] = _rmsnorm(y, nf_ref[...])


def _peer(h2, u_bf16, vt_bf16, cut, e1, rank2, e2, x1, nf):
    T = h2.shape[0]
    tT = PEER_TOK_TILE
    eT = PEER_EXP_TILE
    sel = pl.BlockSpec((PEER_HEADS, PEER_NKEYS, tT), lambda t, e: (0, 0, t))
    return pl.pallas_call(
        _peer_kernel,
        name="peer",
        grid=(T // tT, PEER_EXPERTS // eT),
        in_specs=[pl.BlockSpec((tT, D_MODEL), lambda t, e: (t, 0)),
                  pl.BlockSpec((eT, D_MODEL), lambda t, e: (e, 0)),
                  pl.BlockSpec((D_MODEL, eT), lambda t, e: (0, e)),
                  sel, sel, sel, sel,
                  pl.BlockSpec((tT, D_MODEL), lambda t, e: (t, 0)),
                  pl.BlockSpec((1, D_MODEL), lambda t, e: (0, 0))],
        out_specs=pl.BlockSpec((tT, D_MODEL), lambda t, e: (t, 0)),
        out_shape=jax.ShapeDtypeStruct((T, D_MODEL), F32),
        scratch_shapes=[pltpu.VMEM((D_MODEL, tT), F32),
                        pltpu.VMEM((eT, tT), F32),
                        pltpu.VMEM((eT, tT), BF16)],
        compiler_params=_cparams(("parallel", "arbitrary")),
    )(h2, u_bf16, vt_bf16, cut, e1, rank2, e2, x1, nf)


def _rope_tables(S):
    half = HEAD_DIM // 2
    inv = ROPE_BASE ** (-jnp.arange(half, dtype=F32) / half)
    ang = jnp.arange(S, dtype=F32)[:, None] * inv[None, :]
    c = jnp.cos(ang)
    s = jnp.sin(ang)
    reps = LANES // HEAD_DIM
    cos_t = jnp.tile(jnp.concatenate([c, c], axis=1), (1, reps))
    sin_t = jnp.tile(jnp.concatenate([-s, s], axis=1), (1, reps))
    return cos_t, sin_t


def _trunk(x, w):
    B, S, _ = x.shape
    T = B * S
    x2 = x.reshape(T, D_MODEL)
    p_rwkv, p_ret, p_gate = _inproj(x2, w["norm1_g"], w["w_in"])
    r, v, kap, lw, kd, b, bonus, g = _rwkv_prep(
        p_rwkv.reshape(B, S, RWKV_COLS), w["mu"], w["w0"], w["w2"], w["a0"], w["a2"], w["g2"],
        w["k_k"], w["k_a"], w["r_k"], w["bd"])
    orw = _rwkv_scan(r, v, kap, lw, kd, b)
    cos_t, sin_t = _rope_tables(S)
    oret = _retention(p_ret.reshape(B, S, RET_COLS), cos_t, sin_t, w["lg"])
    x1, h2, sc = _merge(orw.reshape(2, T, WIDTH), bonus.reshape(T, WIDTH), g.reshape(T, WIDTH),
                        oret.reshape(2, T, WIDTH), p_ret, p_gate, x2,
                        w["ln_g"], w["ln_b"], w["ret_ln_g"], w["bd"], w["proj_a"], w["proj_b"], w["w_out"],
                        w["norm2_g"], w["wq"], w["sk"])
    cut, e1, rank2, e2 = _topk_prep(sc)
    y = _peer(h2, w["u"], w["vt"], cut, e1, rank2, e2, x1, w["normf_g"])
    return y.reshape(B, S, D_MODEL)


def kernel(x_prompt, x_sample, norm1_g, w_in, rwkv_mu, rwkv_w0, rwkv_w2, rwkv_a0, rwkv_a2, rwkv_g2, rwkv_k_k,
           rwkv_k_a, rwkv_r_k, rwkv_ln_g, rwkv_ln_b, ret_ln_g, proj_a, proj_b, w_out, norm2_g, peer_wq,
           peer_subkeys, peer_u, peer_v, normf_g):
    assert norm1_g.shape[0] == 1, "single-layer trunk"
    head_id = jnp.arange(WIDTH) // HEAD_DIM
    w = {
        "norm1_g": norm1_g[0][None, :],
        "w_in": w_in[0].astype(BF16),
        "mu": rwkv_mu[0],
        "w0": rwkv_w0[0],
        "w2": rwkv_w2[0].astype(BF16),
        "a0": rwkv_a0[0],
        "a2": rwkv_a2[0].astype(BF16),
        "g2": rwkv_g2[0].astype(BF16),
        "k_k": rwkv_k_k[0][None, :],
        "k_a": rwkv_k_a[0][None, :],
        "r_k": rwkv_r_k[0].reshape(1, WIDTH),
        "ln_g": rwkv_ln_g[0][None, :],
        "ln_b": rwkv_ln_b[0][None, :],
        "ret_ln_g": ret_ln_g[0][None, :],
        "bd": (head_id[:, None] == head_id[None, :]).astype(BF16),
        "lg": jnp.broadcast_to(jnp.log1p(-jnp.exp2(-5.0 - jnp.arange(N_HEADS, dtype=F32)))[:, None],
                               (N_HEADS, LANES)),
        "proj_a": proj_a[0].astype(BF16),
        "proj_b": proj_b[0].astype(BF16),
        "w_out": w_out[0].astype(BF16),
        "norm2_g": norm2_g[0][None, :],
        "wq": peer_wq[0].astype(BF16),
        "sk": peer_subkeys[0].reshape(2 * PEER_HEADS, PEER_NKEYS, 128).astype(BF16),
        "u": peer_u[0].astype(BF16),
        "vt": peer_v[0].T.astype(BF16),
        "normf_g": normf_g[None, :],
    }
    return (_trunk(x_prompt, w), _trunk(x_sample, w))
```

```python
import functools

import jax
import jax.numpy as jnp
from jax import lax
from jax.experimental import pallas as pl
from jax.experimental.pallas import tpu as pltpu

F32 = jnp.float32
BF16 = jnp.bfloat16

D_MODEL = 1024
HEAD_DIM = 64
N_HEADS = 8
WIDTH = N_HEADS * HEAD_DIM
RWKV_COLS = 1920
RET_COLS = 2048
GATE_COLS = 2048
RWKV_GN_EPS = 64e-5
RET_GN_EPS = 1e-6
NORM_EPS = 1e-6
ROPE_BASE = 10000.0
PEER_HEADS = 8
PEER_NKEYS = 128
PEER_TOPK = 16
PEER_EXPERTS = PEER_NKEYS * PEER_NKEYS

LANES = 128
CHUNK = 64
SEQ_TILE = 256
SCAN_GROUPS = 4
TOK_TILE = 256
PEER_TOK_TILE = 512
PEER_EXP_TILE = 2048
TOPK_TILE = 512
VMEM_LIMIT = 56 * 1024 * 1024
NEG_BIG = -1e30

_NT = (((1,), (1,)), ((), ()))


def _cparams(sem):
    return pltpu.CompilerParams(dimension_semantics=sem, vmem_limit_bytes=VMEM_LIMIT)


def _dot(a, b):
    return jnp.dot(a, b, preferred_element_type=F32)


def _dot_nt(a, b):
    return lax.dot_general(a, b, _NT, preferred_element_type=F32)


def _dot_tn(a, b):
    return lax.dot_general(a, b, (((0,), (0,)), ((), ())), preferred_element_type=F32)


def _split2(x):
    hi = x.astype(BF16)
    lo = (x - hi.astype(F32)).astype(BF16)
    return hi, lo


def _ones_mm(m_bf16, x):
    x0 = x.astype(BF16)
    r1 = x - x0.astype(F32)
    x1 = r1.astype(BF16)
    x2 = (r1 - x1.astype(F32)).astype(BF16)
    return _dot(m_bf16, x0) + (_dot(m_bf16, x1) + _dot(m_bf16, x2))


def _segsum(x, bd):
    hi, lo = _split2(x)
    return _dot(hi, bd) + _dot(lo, bd)


def _rmsnorm(x, g):
    return x * lax.rsqrt(jnp.mean(x * x, axis=-1, keepdims=True) + NORM_EPS) * g


def _inproj_kernel(x_ref, g_ref, w_ref, o1_ref, o2_ref, o3_ref):
    hb = _rmsnorm(x_ref[...], g_ref[...]).astype(BF16)
    o1_ref[...] = _dot(hb, w_ref[:, 0:RWKV_COLS])
    o2_ref[...] = _dot(hb, w_ref[:, RWKV_COLS:RWKV_COLS + RET_COLS])
    o3_ref[...] = _dot(hb, w_ref[:, RWKV_COLS + RET_COLS:])


def _inproj(x2, g, w_bf16):
    T = x2.shape[0]
    tm = TOK_TILE
    ncol = w_bf16.shape[1]
    return pl.pallas_call(
        _inproj_kernel,
        name="inproj",
        grid=(T // tm,),
        in_specs=[
            pl.BlockSpec((tm, D_MODEL), lambda i: (i, 0)),
            pl.BlockSpec((1, D_MODEL), lambda i: (0, 0)),
            pl.BlockSpec((D_MODEL, ncol), lambda i: (0, 0)),
        ],
        out_specs=[
            pl.BlockSpec((tm, RWKV_COLS), lambda i: (i, 0)),
            pl.BlockSpec((tm, RET_COLS), lambda i: (i, 0)),
            pl.BlockSpec((tm, GATE_COLS), lambda i: (i, 0)),
        ],
        out_shape=[
            jax.ShapeDtypeStruct((T, RWKV_COLS), F32),
            jax.ShapeDtypeStruct((T, RET_COLS), F32),
            jax.ShapeDtypeStruct((T, GATE_COLS), F32),
        ],
        compiler_params=_cparams(("parallel",)),
    )(x2, g, w_bf16)


def _softplus(y):
    return jnp.maximum(y, 0.0) + jnp.log1p(jnp.exp(-jnp.abs(y)))


def _rwkv_prep_kernel(p_ref, pprev_ref, pnext_ref, mu_ref, w0_ref, w2_ref, a0_ref, a2_ref, g2_ref,
                      kk_ref, ka_ref, rk_ref, bd_ref,
                      r_ref, v_ref, kap_ref, lw_ref, kd_ref, b_ref, bonus_ref, g_ref):
    i = pl.program_id(1)
    n = pl.num_programs(1)
    p = p_ref[0]
    tm = p.shape[0]
    prev_row = jnp.where(i > 0, pprev_ref[0, 7:8, :], 0.0)
    next_row = jnp.where(i < n - 1, pnext_ref[0, 0:1, :], 0.0)
    row = lax.broadcasted_iota(jnp.int32, p.shape, 0)
    prev = jnp.where(row == 0, prev_row, pltpu.roll(p, 1, 0))
    nxt = jnp.where(row == tm - 1, next_row, pltpu.roll(p, tm - 1, 0))
    ps = p + mu_ref[0:1, :] * (prev - p) + mu_ref[1:2, :] * (nxt - p)

    bd = bd_ref[...]
    r = ps[:, 0:WIDTH]
    k = ps[:, WIDTH:2 * WIDTH]
    v = ps[:, 2 * WIDTH:3 * WIDTH]
    kk = k * kk_ref[...]
    kappa = kk / jnp.maximum(jnp.sqrt(_segsum(kk * kk, bd)), 1e-12)
    r_ref[0] = r
    v_ref[0] = v
    kap_ref[0] = kappa
    ksum = jnp.zeros_like(k)
    for d in range(2):
        wl = ps[:, 1536 + 64 * d:1600 + 64 * d]
        al = ps[:, 1664 + 64 * d:1728 + 64 * d]
        z = w0_ref[d:d + 1, :] + _dot(jnp.tanh(wl).astype(BF16), w2_ref[d])
        w_log = -_softplus(-z) - 0.5
        lw_ref[d, 0] = -jnp.exp(w_log)
        a = jax.nn.sigmoid(a0_ref[d:d + 1, :] + _dot(al.astype(BF16), a2_ref[d]))
        kd = k * (1.0 + (a - 1.0) * ka_ref[...])
        kd_ref[d, 0] = kd
        b_ref[d, 0] = kappa * a
        ksum = ksum + kd
    bonus_ref[0] = _segsum(r * (0.5 * ksum) * rk_ref[...], bd) * v
    g_ref[0] = _dot(jax.nn.sigmoid(ps[:, 1792:1920]).astype(BF16), g2_ref[...])


def _rwkv_prep(p_rwkv, mu, w0, w2, a0, a2, g2, k_k, k_a, r_k, bd):
    B, S, _ = p_rwkv.shape
    tm = SEQ_TILE
    n = S // tm
    nb8 = S // 8
    full = lambda shape: pl.BlockSpec(shape, lambda b, i: (0,) * len(shape))
    tok = pl.BlockSpec((1, tm, WIDTH), lambda b, i: (b, i, 0))
    tok2 = pl.BlockSpec((2, 1, tm, WIDTH), lambda b, i: (0, b, i, 0))
    s1 = jax.ShapeDtypeStruct((B, S, WIDTH), F32)
    s2 = jax.ShapeDtypeStruct((2, B, S, WIDTH), F32)
    return pl.pallas_call(
        _rwkv_prep_kernel,
        name="rwkv_prep",
        grid=(B, n),
        in_specs=[
            pl.BlockSpec((1, tm, RWKV_COLS), lambda b, i: (b, i, 0)),
            pl.BlockSpec((1, 8, RWKV_COLS), lambda b, i: (b, jnp.maximum(i * (tm // 8) - 1, 0), 0)),
            pl.BlockSpec((1, 8, RWKV_COLS), lambda b, i: (b, jnp.minimum((i + 1) * (tm // 8), nb8 - 1), 0)),
            full((2, RWKV_COLS)), full((2, WIDTH)), full((2, 64, WIDTH)), full((2, WIDTH)),
            full((2, 64, WIDTH)), full((128, WIDTH)), full((1, WIDTH)), full((1, WIDTH)), full((1, WIDTH)),
            full((WIDTH, WIDTH)),
        ],
        out_specs=[tok, tok, tok, tok2, tok2, tok2, tok, tok],
        out_shape=[s1, s1, s1, s2, s2, s2, s1, s1],
        compiler_params=_cparams(("parallel", "parallel")),
    )(p_rwkv, p_rwkv, p_rwkv, mu, w0, w2, a0, a2, g2, k_k, k_a, r_k, bd)


def _scan_groups(groups, states, fwd, strict, incl, cum_b):
    tm = SEQ_TILE
    nck = tm // CHUNK
    ng = len(groups)
    lane = lax.broadcasted_iota(jnp.int32, (1, LANES), 1)
    gs, RRb, Ath, Rth, Vhb = [], [], [], [], []
    for (R, V, KAP, LW, KD, BV) in groups:
        g = _ones_mm(cum_b, LW)
        eng = jnp.exp(-g)
        At = -KAP * jnp.exp(g - LW)
        Rt = R * jnp.exp(g)
        gs.append(g)
        RRb.append(jnp.concatenate([BV * eng, KD * eng], axis=0).astype(BF16))
        for h in range(2):
            mh = jnp.where((lane // HEAD_DIM) == h, 1.0, 0.0)
            Ath.append(At * mh)
            Rth.append(Rt * mh)
            Vhb.append((V * mh).astype(BF16))
    chains = range(2 * ng)
    Q = [_dot_nt(jnp.concatenate([Ath[c], Rth[c]], axis=0).astype(BF16), RRb[c // 2]) for c in chains]
    Nb = [jnp.where(strict, Q[c][:tm, :tm], 0.0).astype(BF16) for c in chains]
    Aak = [jnp.where(strict, Q[c][:tm, tm:], 0.0).astype(BF16) for c in chains]
    Arb = [jnp.where(incl, Q[c][tm:, :tm], 0.0).astype(BF16) for c in chains]
    Ark = [jnp.where(incl, Q[c][tm:, tm:], 0.0).astype(BF16) for c in chains]
    X = [jnp.concatenate([Ath[c], _dot(Aak[c], Vhb[c])], axis=1) for c in chains]
    X = [X[c] + _dot(Nb[c], X[c].astype(BF16)) for c in chains]
    Pb = Nb
    for _ in range(5):
        Pb = [_dot(Pb[c], Pb[c]).astype(BF16) for c in chains]
        X = [X[c] + _dot(Pb[c], X[c].astype(BF16)) for c in chains]
    Rhh = [Rth[c] + _dot(Arb[c], X[c][:, :LANES].astype(BF16)) for c in chains]
    Ohh = [_dot(Arb[c], X[c][:, LANES:].astype(BF16)) + _dot(Ark[c], Vhb[c]) for c in chains]

    r128 = lax.broadcasted_iota(jnp.int32, (LANES, LANES), 0)
    c128 = lax.broadcasted_iota(jnp.int32, (LANES, LANES), 1)
    bdm = (r128 // HEAD_DIM) == (c128 // HEAD_DIM)
    eye = r128 == c128
    colc = lax.broadcasted_iota(jnp.int32, (1, tm), 1) // CHUNK
    Rh, Oh, Gs, Hs = [], [], [], []
    for gi, (R, V, KAP, LW, KD, BV) in enumerate(groups):
        g = gs[gi]
        Wtb = (X[2 * gi][:, :LANES] + X[2 * gi + 1][:, :LANES]).astype(BF16)
        Utb = (X[2 * gi][:, LANES:] + X[2 * gi + 1][:, LANES:]).astype(BF16)
        Rh.append(Rhh[2 * gi] + Rhh[2 * gi + 1])
        Oh.append(Ohh[2 * gi] + Ohh[2 * gi + 1])
        ends = []
        for c in range(nck):
            e_f = g[c * CHUNK + CHUNK - 1:c * CHUNK + CHUNK, :]
            e_b = g[c * CHUNK:c * CHUNK + 1, :]
            ends.append(jnp.where(fwd, e_f, e_b))
        gend = jnp.concatenate([jnp.broadcast_to(e, (CHUNK, LANES)) for e in ends], axis=0)
        egc = jnp.exp(gend - g)
        BhT = (BV * egc).T.astype(BF16)
        KhT = (KD * egc).T.astype(BF16)
        Vb = V.astype(BF16)
        Gg, Hg = [], []
        for c in range(nck):
            cm = colc == c
            Bc = jnp.where(cm, BhT, jnp.zeros((), BF16))
            Kc = jnp.where(cm, KhT, jnp.zeros((), BF16))
            Gg.append(jnp.where(eye, jnp.exp(ends[c]), 0.0) + jnp.where(bdm, _dot(Bc, Wtb), 0.0))
            Hg.append(jnp.where(bdm, _dot(Bc, Utb) + _dot(Kc, Vb), 0.0))
        Gs.append(Gg)
        Hs.append(Hg)

    St = list(states)
    outs = [[None] * nck for _ in range(ng)]
    for step in range(nck):
        c_f = step
        c_b = nck - 1 - step
        rows_f = slice(c_f * CHUNK, (c_f + 1) * CHUNK)
        rows_b = slice(c_b * CHUNK, (c_b + 1) * CHUNK)
        for gi in range(ng):
            Rc = jnp.where(fwd, Rh[gi][rows_f], Rh[gi][rows_b])
            Oc = jnp.where(fwd, Oh[gi][rows_f], Oh[gi][rows_b])
            Gc = jnp.where(fwd, Gs[gi][c_f], Gs[gi][c_b])
            Hc = jnp.where(fwd, Hs[gi][c_f], Hs[gi][c_b])
            Stb = St[gi].astype(BF16)
            outs[gi][step] = _dot(Rc.astype(BF16), Stb) + Oc
            St[gi] = _dot(Gc.astype(BF16), Stb) + Hc
    res = []
    for gi in range(ng):
        o_f = jnp.concatenate(outs[gi], axis=0)
        o_b = jnp.concatenate(outs[gi][::-1], axis=0)
        res.append(jnp.where(fwd, o_f, o_b))
    return res, St


def _rwkv_scan_kernel(r_ref, v_ref, kap_ref, lw_ref, kd_ref, b_ref, o_ref, st_ref):
    d = pl.program_id(2)
    i = pl.program_id(3)
    tm = SEQ_TILE

    @pl.when(i == 0)
    def _():
        st_ref[...] = jnp.zeros_like(st_ref)

    fwd = d == 0
    ri = lax.broadcasted_iota(jnp.int32, (tm, tm), 0)
    ci = lax.broadcasted_iota(jnp.int32, (tm, tm), 1)
    same = (ri // CHUNK) == (ci // CHUNK)
    ahead = jnp.where(fwd, ri - ci, ci - ri)
    strict = same & (ahead > 0)
    incl = same & (ahead >= 0)
    cum_b = jnp.where(incl, 1.0, 0.0).astype(BF16)
    lanes = [slice(grp * LANES, (grp + 1) * LANES) for grp in range(SCAN_GROUPS)]
    groups = [(r_ref[0, :, ls], v_ref[0, :, ls], kap_ref[0, :, ls], lw_ref[0, 0, :, ls],
               kd_ref[0, 0, :, ls], b_ref[0, 0, :, ls]) for ls in lanes]
    outs, sts = _scan_groups(groups, [st_ref[grp] for grp in range(SCAN_GROUPS)], fwd, strict, incl, cum_b)
    for grp, ls in enumerate(lanes):
        o_ref[0, 0, :, ls] = outs[grp]
        st_ref[grp] = sts[grp]


def _rwkv_scan(r, v, kap, lw, kd, b):
    B, S, _ = r.shape
    tm = SEQ_TILE
    n = S // tm
    wb = SCAN_GROUPS * LANES
    pos = lambda d, i: jnp.where(d == 0, i, n - 1 - i)
    tok = pl.BlockSpec((1, tm, wb), lambda bb, hp, d, i: (bb, pos(d, i), hp))
    tok2 = pl.BlockSpec((1, 1, tm, wb), lambda bb, hp, d, i: (d, bb, pos(d, i), hp))
    return pl.pallas_call(
        _rwkv_scan_kernel,
        name="rwkv_scan",
        grid=(B, WIDTH // wb, 2, n),
        in_specs=[tok, tok, tok, tok2, tok2, tok2],
        out_specs=tok2,
        out_shape=jax.ShapeDtypeStruct((2, B, S, WIDTH), F32),
        scratch_shapes=[pltpu.VMEM((SCAN_GROUPS, LANES, LANES), F32)],
        compiler_params=_cparams(("parallel", "parallel", "parallel", "arbitrary")),
    )(r, v, kap, lw, kd, b)


def _retention_kernel(q_ref, k_ref, v_ref, cos_ref, sin_ref, lg_ref, o_ref, st_ref, xi_ref, zt_ref, dm_ref):
    d = pl.program_id(1)
    i = pl.program_id(2)
    tm = SEQ_TILE
    ng = WIDTH // LANES
    groups = [slice(g * LANES, (g + 1) * LANES) for g in range(ng)]
    fwd = d == 0
    lane = lax.broadcasted_iota(jnp.int32, (1, LANES), 1)
    lgl = [jnp.where(lane < HEAD_DIM, lg_ref[2 * g:2 * g + 1, :], lg_ref[2 * g + 1:2 * g + 2, :])
           for g in range(ng)]

    @pl.when(i == 0)
    def _():
        st_ref[...] = jnp.zeros_like(st_ref)
        rowf = lax.broadcasted_iota(jnp.int32, (tm, LANES), 0).astype(F32)
        e_q = jnp.where(fwd, rowf + 1.0, tm - rowf)
        e_k = jnp.where(fwd, tm - 1.0 - rowf, rowf)
        for g in range(ng):
            xi_ref[:, groups[g]] = jnp.exp(e_q * lgl[g])
            zt_ref[:, groups[g]] = jnp.exp(e_k * lgl[g])

    @pl.when(jnp.logical_and(i == 0, fwd))
    def _():
        ri = lax.broadcasted_iota(jnp.int32, (tm, tm), 0)
        ci = lax.broadcasted_iota(jnp.int32, (tm, tm), 1)
        dist = jnp.abs(ri - ci).astype(F32)
        for h in range(N_HEADS):
            dm_ref[h] = jnp.exp(dist * lg_ref[h:h + 1, 0:1])

    cosv = jnp.concatenate([cos_ref[...]] * ng, axis=1)
    sinv = jnp.concatenate([sin_ref[...]] * ng, axis=1)
    half = HEAD_DIM // 2
    lane_w = lax.broadcasted_iota(jnp.int32, (1, WIDTH), 1)
    first = (lane_w % HEAD_DIM) < half

    def rot(t):
        sw = jnp.where(first, pltpu.roll(t, WIDTH - half, 1), pltpu.roll(t, half, 1))
        return t * cosv + sw * sinv

    q = rot(q_ref[0]) * (HEAD_DIM ** -0.5)
    k = rot(k_ref[0])
    v = v_ref[0]
    qx = (q * xi_ref[...]).astype(BF16)
    kz = (k * zt_ref[...]).astype(BF16)
    vb = v.astype(BF16)

    r128 = lax.broadcasted_iota(jnp.int32, (LANES, LANES), 0)
    c128 = lax.broadcasted_iota(jnp.int32, (LANES, LANES), 1)
    bdm = (r128 // HEAD_DIM) == (c128 // HEAD_DIM)
    St = [st_ref[g] for g in range(ng)]
    cross = [_dot(qx[:, groups[g]], St[g].astype(BF16)) for g in range(ng)]
    kv = [_dot_tn(kz[:, groups[g]], vb[:, groups[g]]) for g in range(ng)]
    for g in range(ng):
        st_ref[g] = jnp.exp(tm * lgl[g]) * St[g] + jnp.where(bdm, kv[g], 0.0)

    @pl.when(fwd)
    def _():
        heads = range(N_HEADS)
        mh = [jnp.where((lane // HEAD_DIM) == h % 2, 1.0, 0.0) for h in heads]
        qh = [(q[:, groups[h // 2]] * mh[h]).astype(BF16) for h in heads]
        vh = [(v[:, groups[h // 2]] * mh[h]).astype(BF16) for h in heads]
        kb = [k[:, groups[g]].astype(BF16) for g in range(ng)]
        s = [_dot_nt(qh[h], kb[h // 2]) for h in heads]
        s = [(s[h] * dm_ref[h]).astype(BF16) for h in heads]
        inner = [_dot(s[h], vh[h]) for h in heads]
        for g in range(ng):
            o_ref[0, 0, :, groups[g]] = cross[g] + inner[2 * g] + inner[2 * g + 1]

    @pl.when(jnp.logical_not(fwd))
    def _():
        for g in range(ng):
            o_ref[0, 0, :, groups[g]] = cross[g]


def _retention(p_ret, cos_t, sin_t, lg):
    B, S, _ = p_ret.shape
    tm = SEQ_TILE
    n = S // tm
    pos = lambda d, i: jnp.where(d == 0, i, n - 1 - i)
    col = lambda j: pl.BlockSpec((1, tm, WIDTH), lambda bb, d, i: (bb, pos(d, i), j))
    tab = pl.BlockSpec((tm, LANES), lambda bb, d, i: (pos(d, i), 0))
    return pl.pallas_call(
        _retention_kernel,
        name="retention",
        grid=(B, 2, n),
        in_specs=[col(0), col(1), col(2), tab, tab,
                  pl.BlockSpec((N_HEADS, LANES), lambda bb, d, i: (0, 0))],
        out_specs=pl.BlockSpec((1, 1, tm, WIDTH), lambda bb, d, i: (d, bb, pos(d, i), 0)),
        out_shape=jax.ShapeDtypeStruct((2, B, S, WIDTH), F32),
        scratch_shapes=[pltpu.VMEM((WIDTH // LANES, LANES, LANES), F32),
                        pltpu.VMEM((tm, WIDTH), F32), pltpu.VMEM((tm, WIDTH), F32),
                        pltpu.VMEM((N_HEADS, tm, tm), F32)],
        compiler_params=_cparams(("parallel", "parallel", "arbitrary")),
    )(p_ret, p_ret, p_ret, cos_t, sin_t, lg)


def _head_norm(o, bd, eps):
    mu = _segsum(o, bd) * (1.0 / HEAD_DIM)
    dlt = o - mu
    var = _segsum(dlt * dlt, bd) * (1.0 / HEAD_DIM)
    return dlt * lax.rsqrt(var + eps)


def _merge_kernel(orw_ref, bonus_ref, g_ref, oret_ref, qg_ref, gate_ref, x_ref,
                  lng_ref, lnb_ref, rlng_ref, bd_ref, pa_ref, pb_ref, wo_ref, n2_ref, wq_ref, sk_ref,
                  x1_ref, h2_ref, sc_ref):
    bd = bd_ref[...]
    o = _head_norm(orw_ref[0] + orw_ref[1], bd, RWKV_GN_EPS) * lng_ref[...] + lnb_ref[...]
    o_a = (o + bonus_ref[...]) * g_ref[...]
    o_b = _head_norm(oret_ref[0] + oret_ref[1], bd, RET_GN_EPS) * rlng_ref[...] * jax.nn.silu(qg_ref[...])
    y_a = _dot(o_a.astype(BF16), pa_ref[...])
    y_b = _dot(o_b.astype(BF16), pb_ref[...])
    gate = gate_ref[...]
    mixed = jax.nn.sigmoid(gate[:, :D_MODEL]) * y_a + jax.nn.sigmoid(gate[:, D_MODEL:]) * y_b
    x1 = x_ref[...] + _dot(mixed.astype(BF16), wo_ref[...])
    x1_ref[...] = x1
    h2 = _rmsnorm(x1, n2_ref[...])
    hb = h2.astype(BF16)
    h2_ref[...] = hb
    q = _dot(hb, wq_ref[...]).astype(BF16)
    for j in range(2 * PEER_HEADS):
        sc_ref[j] = _dot_nt(sk_ref[j], q[:, j * 128:(j + 1) * 128])


def _merge(orw, bonus, g, oret, p_ret, p_gate, x2, lng, lnb, rlng, bd, pa, pb, wo, n2, wq, sk):
    T = x2.shape[0]
    tm = TOK_TILE
    full = lambda shape: pl.BlockSpec(shape, lambda i: (0,) * len(shape))
    tok = lambda w: pl.BlockSpec((tm, w), lambda i: (i, 0))
    two = pl.BlockSpec((2, tm, WIDTH), lambda i: (0, i, 0))
    return pl.pallas_call(
        _merge_kernel,
        name="merge",
        grid=(T // tm,),
        in_specs=[two, tok(WIDTH), tok(WIDTH), two,
                  pl.BlockSpec((tm, WIDTH), lambda i: (i, 3)),
                  tok(GATE_COLS), tok(D_MODEL),
                  full((1, WIDTH)), full((1, WIDTH)), full((1, WIDTH)), full((WIDTH, WIDTH)),
                  full((WIDTH, D_MODEL)), full((WIDTH, D_MODEL)), full((D_MODEL, D_MODEL)), full((1, D_MODEL)),
                  full((D_MODEL, 2 * PEER_HEADS * 128)), full((2 * PEER_HEADS, 128, 128))],
        out_specs=[tok(D_MODEL), tok(D_MODEL),
                   pl.BlockSpec((2 * PEER_HEADS, PEER_NKEYS, tm), lambda i: (0, 0, i))],
        out_shape=[jax.ShapeDtypeStruct((T, D_MODEL), F32),
                   jax.ShapeDtypeStruct((T, D_MODEL), BF16),
                   jax.ShapeDtypeStruct((2 * PEER_HEADS, PEER_NKEYS, T), F32)],
        compiler_params=_cparams(("parallel",)),
    )(orw, bonus, g, oret, p_ret, p_gate, x2, lng, lnb, rlng, bd, pa, pb, wo, n2, wq, sk)


def _top16(s):
    rows16 = lax.broadcasted_iota(jnp.int32, (PEER_TOPK, s.shape[1]), 0)
    vals = jnp.zeros((PEER_TOPK, s.shape[1]), F32)
    for a in range(PEER_TOPK):
        m = jnp.max(s, axis=0, keepdims=True)
        vals = jnp.where(rows16 == a, m, vals)
        s = jnp.where(s == m, NEG_BIG, s)
    return vals


def _topk_prep_kernel(sc_ref, c_ref, e1_ref, rk_ref, e2_ref):
    s1 = sc_ref[0]
    s2 = sc_ref[1]
    v1 = _top16(s1)
    v2 = _top16(s2)
    tl = s1.shape[1]
    cands = [v1[0:1, :] + v2]
    for a in range(1, 8):
        cands.append(v1[a:a + 1, :] + v2[0:8, :])
    cands.append(v1[8:16, :] + v2[0:1, :])
    tau = _top16(jnp.concatenate(cands, axis=0))[PEER_TOPK - 1:PEER_TOPK, :]

    cut = jnp.zeros_like(s1)
    rank2 = jnp.zeros_like(s2)
    cut16 = jnp.zeros_like(v1)
    for b in range(PEER_TOPK):
        vb = v2[b:b + 1, :]
        cut = jnp.where(s1 + vb >= tau, b + 1.0, cut)
        rank2 = jnp.where(vb > s2, b + 1.0, rank2)
        cut16 = jnp.where(v1 + vb >= tau, b + 1.0, cut16)
    e1v = jnp.exp(v1 - v1[0:1, :])
    e2v = jnp.exp(v2 - v2[0:1, :])
    zacc = jnp.zeros_like(v1)
    for b in range(PEER_TOPK):
        zacc = zacc + jnp.where(cut16 > b, e1v, 0.0) * e2v[b:b + 1, :]
    z = jnp.sum(zacc, axis=0, keepdims=True)
    c_ref[0] = cut
    e1_ref[0] = jnp.exp(s1 - v1[0:1, :]) * (0.5 / z)
    rk_ref[0] = rank2.astype(BF16)
    e2_ref[0] = jnp.exp(s2 - v2[0:1, :]).astype(BF16)
    del tl


def _topk_prep(sc):
    _, nk, T = sc.shape
    tl = TOPK_TILE
    blk = pl.BlockSpec((1, nk, tl), lambda h, t: (h, 0, t))
    sf = jax.ShapeDtypeStruct((PEER_HEADS, nk, T), F32)
    sb = jax.ShapeDtypeStruct((PEER_HEADS, nk, T), BF16)
    return pl.pallas_call(
        _topk_prep_kernel,
        name="topk_prep",
        grid=(PEER_HEADS, T // tl),
        in_specs=[pl.BlockSpec((2, nk, tl), lambda h, t: (h, 0, t))],
        out_specs=[blk, blk, blk, blk],
        out_shape=[sf, sf, sb, sb],
        compiler_params=_cparams(("parallel", "parallel")),
    )(sc)


def _gelu2(x):
    return x * (1.0 + lax.erf(x * (2.0 ** -0.5)))


def _peer_kernel(h_ref, u_ref, vt_ref, c_ref, e1_ref, rk_ref, e2_ref, x1_ref, nf_ref, y_ref,
                 acc_ref, act_ref, p_ref):
    e = pl.program_id(1)
    ne = pl.num_programs(1)
    nrow = PEER_EXP_TILE // PEER_NKEYS

    @pl.when(e == 0)
    def _():
        acc_ref[...] = jnp.zeros_like(acc_ref)

    act_ref[...] = _dot_nt(u_ref[...], h_ref[...])
    base = pl.multiple_of(e * nrow, nrow)
    cuts = [c_ref[h, pl.ds(base, nrow), :] for h in range(PEER_HEADS)]
    e1s = [e1_ref[h, pl.ds(base, nrow), :] for h in range(PEER_HEADS)]
    for il in range(nrow):
        rows = slice(il * PEER_NKEYS, (il + 1) * PEER_NKEYS)
        w = None
        for h in range(PEER_HEADS):
            cut = cuts[h][il:il + 1, :].astype(BF16)
            e1 = e1s[h][il:il + 1, :].astype(BF16)
            term = jnp.where(rk_ref[h] < cut, e2_ref[h], jnp.zeros((), BF16)) * e1
            w = term if w is None else w + term
        p_ref[rows, :] = w * _gelu2(act_ref[rows, :]).astype(BF16)
    acc_ref[...] += _dot(vt_ref[...], p_ref[...])

    @pl.when(e == ne - 1)
    def _():
        y = x1_ref[...] + acc_ref[...].T
        y_ref[...] = _rmsnorm(y, nf_ref[...])


def _peer(h2, u_bf16, vt_bf16, cut, e1, rank2, e2, x1, nf):
    T = h2.shape[0]
    tT = PEER_TOK_TILE
    eT = PEER_EXP_TILE
    sel = pl.BlockSpec((PEER_HEADS, PEER_NKEYS, tT), lambda t, e: (0, 0, t))
    return pl.pallas_call(
        _peer_kernel,
        name="peer",
        grid=(T // tT, PEER_EXPERTS // eT),
        in_specs=[pl.BlockSpec((tT, D_MODEL), lambda t, e: (t, 0)),
                  pl.BlockSpec((eT, D_MODEL), lambda t, e: (e, 0)),
                  pl.BlockSpec((D_MODEL, eT), lambda t, e: (0, e)),
                  sel, sel, sel, sel,
                  pl.BlockSpec((tT, D_MODEL), lambda t, e: (t, 0)),
                  pl.BlockSpec((1, D_MODEL), lambda t, e: (0, 0))],
        out_specs=pl.BlockSpec((tT, D_MODEL), lambda t, e: (t, 0)),
        out_shape=jax.ShapeDtypeStruct((T, D_MODEL), F32),
        scratch_shapes=[pltpu.VMEM((D_MODEL, tT), F32),
                        pltpu.VMEM((eT, tT), F32),
                        pltpu.VMEM((eT, tT), BF16)],
        compiler_params=_cparams(("parallel", "arbitrary")),
    )(h2, u_bf16, vt_bf16, cut, e1, rank2, e2, x1, nf)


def _rope_tables(S):
    half = HEAD_DIM // 2
    inv = ROPE_BASE ** (-jnp.arange(half, dtype=F32) / half)
    ang = jnp.arange(S, dtype=F32)[:, None] * inv[None, :]
    c = jnp.cos(ang)
    s = jnp.sin(ang)
    reps = LANES // HEAD_DIM
    cos_t = jnp.tile(jnp.concatenate([c, c], axis=1), (1, reps))
    sin_t = jnp.tile(jnp.concatenate([-s, s], axis=1), (1, reps))
    return cos_t, sin_t


def _trunk(x, w):
    B, S, _ = x.shape
    T = B * S
    x2 = x.reshape(T, D_MODEL)
    p_rwkv, p_ret, p_gate = _inproj(x2, w["norm1_g"], w["w_in"])
    r, v, kap, lw, kd, b, bonus, g = _rwkv_prep(
        p_rwkv.reshape(B, S, RWKV_COLS), w["mu"], w["w0"], w["w2"], w["a0"], w["a2"], w["g2"],
        w["k_k"], w["k_a"], w["r_k"], w["bd"])
    orw = _rwkv_scan(r, v, kap, lw, kd, b)
    cos_t, sin_t = _rope_tables(S)
    oret = _retention(p_ret.reshape(B, S, RET_COLS), cos_t, sin_t, w["lg"])
    x1, h2, sc = _merge(orw.reshape(2, T, WIDTH), bonus.reshape(T, WIDTH), g.reshape(T, WIDTH),
                        oret.reshape(2, T, WIDTH), p_ret, p_gate, x2,
                        w["ln_g"], w["ln_b"], w["ret_ln_g"], w["bd"], w["proj_a"], w["proj_b"], w["w_out"],
                        w["norm2_g"], w["wq"], w["sk"])
    cut, e1, rank2, e2 = _topk_prep(sc)
    y = _peer(h2, w["u"], w["vt"], cut, e1, rank2, e2, x1, w["normf_g"])
    return y.reshape(B, S, D_MODEL)


def kernel(x_prompt, x_sample, norm1_g, w_in, rwkv_mu, rwkv_w0, rwkv_w2, rwkv_a0, rwkv_a2, rwkv_g2, rwkv_k_k,
           rwkv_k_a, rwkv_r_k, rwkv_ln_g, rwkv_ln_b, ret_ln_g, proj_a, proj_b, w_out, norm2_g, peer_wq,
           peer_subkeys, peer_u, peer_v, normf_g):
    assert norm1_g.shape[0] == 1, "single-layer trunk"
    head_id = jnp.arange(WIDTH) // HEAD_DIM
    w = {
        "norm1_g": norm1_g[0][None, :],
        "w_in": w_in[0].astype(BF16),
        "mu": rwkv_mu[0],
        "w0": rwkv_w0[0],
        "w2": rwkv_w2[0].astype(BF16),
        "a0": rwkv_a0[0],
        "a2": rwkv_a2[0].astype(BF16),
        "g2": rwkv_g2[0].astype(BF16),
        "k_k": rwkv_k_k[0][None, :],
        "k_a": rwkv_k_a[0][None, :],
        "r_k": rwkv_r_k[0].reshape(1, WIDTH),
        "ln_g": rwkv_ln_g[0][None, :],
        "ln_b": rwkv_ln_b[0][None, :],
        "ret_ln_g": ret_ln_g[0][None, :],
        "bd": (head_id[:, None] == head_id[None, :]).astype(BF16),
        "lg": jnp.broadcast_to(jnp.log1p(-jnp.exp2(-5.0 - jnp.arange(N_HEADS, dtype=F32)))[:, None],
                               (N_HEADS, LANES)),
        "proj_a": proj_a[0].astype(BF16),
        "proj_b": proj_b[0].astype(BF16),
        "w_out": w_out[0].astype(BF16),
        "norm2_g": norm2_g[0][None, :],
        "wq": peer_wq[0].astype(BF16),
        "sk": peer_subkeys[0].reshape(2 * PEER_HEADS, PEER_NKEYS, 128).astype(BF16),
        "u": peer_u[0].astype(BF16),
        "vt": peer_v[0].T.astype(BF16),
        "normf_g": normf_g[None, :],
    }
    return (_trunk(x_prompt, w), _trunk(x_sample, w))
```

```python
import functools

import jax
import jax.numpy as jnp
from jax import lax
from jax.experimental import pallas as pl
from jax.experimental.pallas import tpu as pltpu

F32 = jnp.float32
BF16 = jnp.bfloat16

D_MODEL = 1024
HEAD_DIM = 64
N_HEADS = 8
WIDTH = N_HEADS * HEAD_DIM
RWKV_COLS = 1920
RET_COLS = 2048
GATE_COLS = 2048
RWKV_GN_EPS = 64e-5
RET_GN_EPS = 1e-6
NORM_EPS = 1e-6
ROPE_BASE = 10000.0
PEER_HEADS = 8
PEER_NKEYS = 128
PEER_TOPK = 16
PEER_EXPERTS = PEER_NKEYS * PEER_NKEYS

LANES = 128
CHUNK = 64
SEQ_TILE = 256
SCAN_GROUPS = 4
TOK_TILE = 256
PEER_TOK_TILE = 512
PEER_EXP_TILE = 2048
TOPK_TILE = 512
VMEM_LIMIT = 56 * 1024 * 1024
NEG_BIG = -1e30

_NT = (((1,), (1,)), ((), ()))


def _cparams(sem):
    return pltpu.CompilerParams(dimension_semantics=sem, vmem_limit_bytes=VMEM_LIMIT)


def _dot(a, b):
    return jnp.dot(a, b, preferred_element_type=F32)


def _dot_nt(a, b):
    return lax.dot_general(a, b, _NT, preferred_element_type=F32)


def _dot_tn(a, b):
    return lax.dot_general(a, b, (((0,), (0,)), ((), ())), preferred_element_type=F32)


def _split2(x):
    hi = x.astype(BF16)
    lo = (x - hi.astype(F32)).astype(BF16)
    return hi, lo


def _ones_mm(m_bf16, x):
    x0 = x.astype(BF16)
    r1 = x - x0.astype(F32)
    x1 = r1.astype(BF16)
    x2 = (r1 - x1.astype(F32)).astype(BF16)
    return _dot(m_bf16, x0) + (_dot(m_bf16, x1) + _dot(m_bf16, x2))


def _segsum(x, bd):
    hi, lo = _split2(x)
    return _dot(hi, bd) + _dot(lo, bd)


def _rmsnorm(x, g):
    return x * lax.rsqrt(jnp.mean(x * x, axis=-1, keepdims=True) + NORM_EPS) * g


def _inproj_kernel(x_ref, g_ref, w_ref, o1_ref, o2_ref, o3_ref):
    hb = _rmsnorm(x_ref[...], g_ref[...]).astype(BF16)
    o1_ref[...] = _dot(hb, w_ref[:, 0:RWKV_COLS])
    o2_ref[...] = _dot(hb, w_ref[:, RWKV_COLS:RWKV_COLS + RET_COLS])
    o3_ref[...] = _dot(hb, w_ref[:, RWKV_COLS + RET_COLS:])


def _inproj(x2, g, w_bf16):
    T = x2.shape[0]
    tm = TOK_TILE
    ncol = w_bf16.shape[1]
    return pl.pallas_call(
        _inproj_kernel,
        name="inproj",
        grid=(T // tm,),
        in_specs=[
            pl.BlockSpec((tm, D_MODEL), lambda i: (i, 0)),
            pl.BlockSpec((1, D_MODEL), lambda i: (0, 0)),
            pl.BlockSpec((D_MODEL, ncol), lambda i: (0, 0)),
        ],
        out_specs=[
            pl.BlockSpec((tm, RWKV_COLS), lambda i: (i, 0)),
            pl.BlockSpec((tm, RET_COLS), lambda i: (i, 0)),
            pl.BlockSpec((tm, GATE_COLS), lambda i: (i, 0)),
        ],
        out_shape=[
            jax.ShapeDtypeStruct((T, RWKV_COLS), F32),
            jax.ShapeDtypeStruct((T, RET_COLS), F32),
            jax.ShapeDtypeStruct((T, GATE_COLS), F32),
        ],
        compiler_params=_cparams(("parallel",)),
    )(x2, g, w_bf16)


def _softplus(y):
    return jnp.maximum(y, 0.0) + jnp.log1p(jnp.exp(-jnp.abs(y)))


def _rwkv_prep_kernel(p_ref, pprev_ref, pnext_ref, mu_ref, w0_ref, w2_ref, a0_ref, a2_ref, g2_ref,
                      kk_ref, ka_ref, rk_ref, bd_ref,
                      r_ref, v_ref, kap_ref, lw_ref, kd_ref, b_ref, bonus_ref, g_ref):
    i = pl.program_id(1)
    n = pl.num_programs(1)
    p = p_ref[0]
    tm = p.shape[0]
    prev_row = jnp.where(i > 0, pprev_ref[0, 7:8, :], 0.0)
    next_row = jnp.where(i < n - 1, pnext_ref[0, 0:1, :], 0.0)
    row = lax.broadcasted_iota(jnp.int32, p.shape, 0)
    prev = jnp.where(row == 0, prev_row, pltpu.roll(p, 1, 0))
    nxt = jnp.where(row == tm - 1, next_row, pltpu.roll(p, tm - 1, 0))
    ps = p + mu_ref[0:1, :] * (prev - p) + mu_ref[1:2, :] * (nxt - p)

    bd = bd_ref[...]
    r = ps[:, 0:WIDTH]
    k = ps[:, WIDTH:2 * WIDTH]
    v = ps[:, 2 * WIDTH:3 * WIDTH]
    kk = k * kk_ref[...]
    kappa = kk / jnp.maximum(jnp.sqrt(_segsum(kk * kk, bd)), 1e-12)
    r_ref[0] = r
    v_ref[0] = v
    kap_ref[0] = kappa
    ksum = jnp.zeros_like(k)
    for d in range(2):
        wl = ps[:, 1536 + 64 * d:1600 + 64 * d]
        al = ps[:, 1664 + 64 * d:1728 + 64 * d]
        z = w0_ref[d:d + 1, :] + _dot(jnp.tanh(wl).astype(BF16), w2_ref[d])
        w_log = -_softplus(-z) - 0.5
        lw_ref[d, 0] = -jnp.exp(w_log)
        a = jax.nn.sigmoid(a0_ref[d:d + 1, :] + _dot(al.astype(BF16), a2_ref[d]))
        kd = k * (1.0 + (a - 1.0) * ka_ref[...])
        kd_ref[d, 0] = kd
        b_ref[d, 0] = kappa * a
        ksum = ksum + kd
    bonus_ref[0] = _segsum(r * (0.5 * ksum) * rk_ref[...], bd) * v
    g_ref[0] = _dot(jax.nn.sigmoid(ps[:, 1792:1920]).astype(BF16), g2_ref[...])


def _rwkv_prep(p_rwkv, mu, w0, w2, a0, a2, g2, k_k, k_a, r_k, bd):
    B, S, _ = p_rwkv.shape
    tm = SEQ_TILE
    n = S // tm
    nb8 = S // 8
    full = lambda shape: pl.BlockSpec(shape, lambda b, i: (0,) * len(shape))
    tok = pl.BlockSpec((1, tm, WIDTH), lambda b, i: (b, i, 0))
    tok2 = pl.BlockSpec((2, 1, tm, WIDTH), lambda b, i: (0, b, i, 0))
    s1 = jax.ShapeDtypeStruct((B, S, WIDTH), F32)
    s2 = jax.ShapeDtypeStruct((2, B, S, WIDTH), F32)
    return pl.pallas_call(
        _rwkv_prep_kernel,
        name="rwkv_prep",
        grid=(B, n),
        in_specs=[
            pl.BlockSpec((1, tm, RWKV_COLS), lambda b, i: (b, i, 0)),
            pl.BlockSpec((1, 8, RWKV_COLS), lambda b, i: (b, jnp.maximum(i * (tm // 8) - 1, 0), 0)),
            pl.BlockSpec((1, 8, RWKV_COLS), lambda b, i: (b, jnp.minimum((i + 1) * (tm // 8), nb8 - 1), 0)),
            full((2, RWKV_COLS)), full((2, WIDTH)), full((2, 64, WIDTH)), full((2, WIDTH)),
            full((2, 64, WIDTH)), full((128, WIDTH)), full((1, WIDTH)), full((1, WIDTH)), full((1, WIDTH)),
            full((WIDTH, WIDTH)),
        ],
        out_specs=[tok, tok, tok, tok2, tok2, tok2, tok, tok],
        out_shape=[s1, s1, s1, s2, s2, s2, s1, s1],
        compiler_params=_cparams(("parallel", "parallel")),
    )(p_rwkv, p_rwkv, p_rwkv, mu, w0, w2, a0, a2, g2, k_k, k_a, r_k, bd)


def _scan_groups(groups, states, fwd, strict, incl, cum_b):
    tm = SEQ_TILE
    nck = tm // CHUNK
    ng = len(groups)
    lane = lax.broadcasted_iota(jnp.int32, (1, LANES), 1)
    gs, RRb, Ath, Rth, Vhb = [], [], [], [], []
    for (R, V, KAP, LW, KD, BV) in groups:
        g = _ones_mm(cum_b, LW)
        eng = jnp.exp(-g)
        At = -KAP * jnp.exp(g - LW)
        Rt = R * jnp.exp(g)
        gs.append(g)
        RRb.append(jnp.concatenate([BV * eng, KD * eng], axis=0).astype(BF16))
        for h in range(2):
            mh = jnp.where((lane // HEAD_DIM) == h, 1.0, 0.0)
            Ath.append(At * mh)
            Rth.append(Rt * mh)
            Vhb.append((V * mh).astype(BF16))
    chains = range(2 * ng)
    Q = [_dot_nt(jnp.concatenate([Ath[c], Rth[c]], axis=0).astype(BF16), RRb[c // 2]) for c in chains]
    Nb = [jnp.where(strict, Q[c][:tm, :tm], 0.0).astype(BF16) for c in chains]
    Aak = [jnp.where(strict, Q[c][:tm, tm:], 0.0).astype(BF16) for c in chains]
    Arb = [jnp.where(incl, Q[c][tm:, :tm], 0.0).astype(BF16) for c in chains]
    Ark = [jnp.where(incl, Q[c][tm:, tm:], 0.0).astype(BF16) for c in chains]
    X = [jnp.concatenate([Ath[c], _dot(Aak[c], Vhb[c])], axis=1) for c in chains]
    X = [X[c] + _dot(Nb[c], X[c].astype(BF16)) for c in chains]
    Pb = Nb
    for _ in range(5):
        Pb = [_dot(Pb[c], Pb[c]).astype(BF16) for c in chains]
        X = [X[c] + _dot(Pb[c], X[c].astype(BF16)) for c in chains]
    Rhh = [Rth[c] + _dot(Arb[c], X[c][:, :LANES].astype(BF16)) for c in chains]
    Ohh = [_dot(Arb[c], X[c][:, LANES:].astype(BF16)) + _dot(Ark[c], Vhb[c]) for c in chains]

    r128 = lax.broadcasted_iota(jnp.int32, (LANES, LANES), 0)
    c128 = lax.broadcasted_iota(jnp.int32, (LANES, LANES), 1)
    bdm = (r128 // HEAD_DIM) == (c128 // HEAD_DIM)
    eye = r128 == c128
    colc = lax.broadcasted_iota(jnp.int32, (1, tm), 1) // CHUNK
    Rh, Oh, Gs, Hs = [], [], [], []
    for gi, (R, V, KAP, LW, KD, BV) in enumerate(groups):
        g = gs[gi]
        Wtb = (X[2 * gi][:, :LANES] + X[2 * gi + 1][:, :LANES]).astype(BF16)
        Utb = (X[2 * gi][:, LANES:] + X[2 * gi + 1][:, LANES:]).astype(BF16)
        Rh.append(Rhh[2 * gi] + Rhh[2 * gi + 1])
        Oh.append(Ohh[2 * gi] + Ohh[2 * gi + 1])
        ends = []
        for c in range(nck):
            e_f = g[c * CHUNK + CHUNK - 1:c * CHUNK + CHUNK, :]
            e_b = g[c * CHUNK:c * CHUNK + 1, :]
            ends.append(jnp.where(fwd, e_f, e_b))
        gend = jnp.concatenate([jnp.broadcast_to(e, (CHUNK, LANES)) for e in ends], axis=0)
        egc = jnp.exp(gend - g)
        BhT = (BV * egc).T.astype(BF16)
        KhT = (KD * egc).T.astype(BF16)
        Vb = V.astype(BF16)
        Gg, Hg = [], []
        for c in range(nck):
            cm = colc == c
            Bc = jnp.where(cm, BhT, jnp.zeros((), BF16))
            Kc = jnp.where(cm, KhT, jnp.zeros((), BF16))
            Gg.append(jnp.where(eye, jnp.exp(ends[c]), 0.0) + jnp.where(bdm, _dot(Bc, Wtb), 0.0))
            Hg.append(jnp.where(bdm, _dot(Bc, Utb) + _dot(Kc, Vb), 0.0))
        Gs.append(Gg)
        Hs.append(Hg)

    St = list(states)
    outs = [[None] * nck for _ in range(ng)]
    for step in range(nck):
        c_f = step
        c_b = nck - 1 - step
        rows_f = slice(c_f * CHUNK, (c_f + 1) * CHUNK)
        rows_b = slice(c_b * CHUNK, (c_b + 1) * CHUNK)
        for gi in range(ng):
            Rc = jnp.where(fwd, Rh[gi][rows_f], Rh[gi][rows_b])
            Oc = jnp.where(fwd, Oh[gi][rows_f], Oh[gi][rows_b])
            Gc = jnp.where(fwd, Gs[gi][c_f], Gs[gi][c_b])
            Hc = jnp.where(fwd, Hs[gi][c_f], Hs[gi][c_b])
            Stb = St[gi].astype(BF16)
            outs[gi][step] = _dot(Rc.astype(BF16), Stb) + Oc
            St[gi] = _dot(Gc.astype(BF16), Stb) + Hc
    res = []
    for gi in range(ng):
        o_f = jnp.concatenate(outs[gi], axis=0)
        o_b = jnp.concatenate(outs[gi][::-1], axis=0)
        res.append(jnp.where(fwd, o_f, o_b))
    return res, St


def _rwkv_scan_kernel(r_ref, v_ref, kap_ref, lw_ref, kd_ref, b_ref, o_ref, st_ref):
    d = pl.program_id(2)
    i = pl.program_id(3)
    tm = SEQ_TILE

    @pl.when(i == 0)
    def _():
        st_ref[...] = jnp.zeros_like(st_ref)

    fwd = d == 0
    ri = lax.broadcasted_iota(jnp.int32, (tm, tm), 0)
    ci = lax.broadcasted_iota(jnp.int32, (tm, tm), 1)
    same = (ri // CHUNK) == (ci // CHUNK)
    ahead = jnp.where(fwd, ri - ci, ci - ri)
    strict = same & (ahead > 0)
    incl = same & (ahead >= 0)
    cum_b = jnp.where(incl, 1.0, 0.0).astype(BF16)
    lanes = [slice(grp * LANES, (grp + 1) * LANES) for grp in range(SCAN_GROUPS)]
    groups = [(r_ref[0, :, ls], v_ref[0, :, ls], kap_ref[0, :, ls], lw_ref[0, 0, :, ls],
               kd_ref[0, 0, :, ls], b_ref[0, 0, :, ls]) for ls in lanes]
    outs, sts = _scan_groups(groups, [st_ref[grp] for grp in range(SCAN_GROUPS)], fwd, strict, incl, cum_b)
    for grp, ls in enumerate(lanes):
        o_ref[0, 0, :, ls] = outs[grp]
        st_ref[grp] = sts[grp]


def _rwkv_scan(r, v, kap, lw, kd, b):
    B, S, _ = r.shape
    tm = SEQ_TILE
    n = S // tm
    wb = SCAN_GROUPS * LANES
    pos = lambda d, i: jnp.where(d == 0, i, n - 1 - i)
    tok = pl.BlockSpec((1, tm, wb), lambda bb, hp, d, i: (bb, pos(d, i), hp))
    tok2 = pl.BlockSpec((1, 1, tm, wb), lambda bb, hp, d, i: (d, bb, pos(d, i), hp))
    return pl.pallas_call(
        _rwkv_scan_kernel,
        name="rwkv_scan",
        grid=(B, WIDTH // wb, 2, n),
        in_specs=[tok, tok, tok, tok2, tok2, tok2],
        out_specs=tok2,
        out_shape=jax.ShapeDtypeStruct((2, B, S, WIDTH), F32),
        scratch_shapes=[pltpu.VMEM((SCAN_GROUPS, LANES, LANES), F32)],
        compiler_params=_cparams(("parallel", "parallel", "parallel", "arbitrary")),
    )(r, v, kap, lw, kd, b)


def _retention_kernel(q_ref, k_ref, v_ref, cos_ref, sin_ref, lg_ref, o_ref, st_ref, xi_ref, zt_ref, dm_ref):
    d = pl.program_id(1)
    i = pl.program_id(2)
    tm = SEQ_TILE
    ng = WIDTH // LANES
    groups = [slice(g * LANES, (g + 1) * LANES) for g in range(ng)]
    fwd = d == 0
    lane = lax.broadcasted_iota(jnp.int32, (1, LANES), 1)
    lgl = [jnp.where(lane < HEAD_DIM, lg_ref[2 * g:2 * g + 1, :], lg_ref[2 * g + 1:2 * g + 2, :])
           for g in range(ng)]

    @pl.when(i == 0)
    def _():
        st_ref[...] = jnp.zeros_like(st_ref)
        rowf = lax.broadcasted_iota(jnp.int32, (tm, LANES), 0).astype(F32)
        e_q = jnp.where(fwd, rowf + 1.0, tm - rowf)
        e_k = jnp.where(fwd, tm - 1.0 - rowf, rowf)
        for g in range(ng):
            xi_ref[:, groups[g]] = jnp.exp(e_q * lgl[g])
            zt_ref[:, groups[g]] = jnp.exp(e_k * lgl[g])

    @pl.when(jnp.logical_and(i == 0, fwd))
    def _():
        ri = lax.broadcasted_iota(jnp.int32, (tm, tm), 0)
        ci = lax.broadcasted_iota(jnp.int32, (tm, tm), 1)
        dist = jnp.abs(ri - ci).astype(F32)
        for h in range(N_HEADS):
            dm_ref[h] = jnp.exp(dist * lg_ref[h:h + 1, 0:1])

    cosv = jnp.concatenate([cos_ref[...]] * ng, axis=1)
    sinv = jnp.concatenate([sin_ref[...]] * ng, axis=1)
    half = HEAD_DIM // 2
    lane_w = lax.broadcasted_iota(jnp.int32, (1, WIDTH), 1)
    first = (lane_w % HEAD_DIM) < half

    def rot(t):
        sw = jnp.where(first, pltpu.roll(t, WIDTH - half, 1), pltpu.roll(t, half, 1))
        return t * cosv + sw * sinv

    q = rot(q_ref[0]) * (HEAD_DIM ** -0.5)
    k = rot(k_ref[0])
    v = v_ref[0]
    qx = (q * xi_ref[...]).astype(BF16)
    kz = (k * zt_ref[...]).astype(BF16)
    vb = v.astype(BF16)

    r128 = lax.broadcasted_iota(jnp.int32, (LANES, LANES), 0)
    c128 = lax.broadcasted_iota(jnp.int32, (LANES, LANES), 1)
    bdm = (r128 // HEAD_DIM) == (c128 // HEAD_DIM)
    St = [st_ref[g] for g in range(ng)]
    cross = [_dot(qx[:, groups[g]], St[g].astype(BF16)) for g in range(ng)]
    kv = [_dot_tn(kz[:, groups[g]], vb[:, groups[g]]) for g in range(ng)]
    for g in range(ng):
        st_ref[g] = jnp.exp(tm * lgl[g]) * St[g] + jnp.where(bdm, kv[g], 0.0)

    @pl.when(fwd)
    def _():
        heads = range(N_HEADS)
        mh = [jnp.where((lane // HEAD_DIM) == h % 2, 1.0, 0.0) for h in heads]
        qh = [(q[:, groups[h // 2]] * mh[h]).astype(BF16) for h in heads]
        vh = [(v[:, groups[h // 2]] * mh[h]).astype(BF16) for h in heads]
        kb = [k[:, groups[g]].astype(BF16) for g in range(ng)]
        s = [_dot_nt(qh[h], kb[h // 2]) for h in heads]
        s = [(s[h] * dm_ref[h]).astype(BF16) for h in heads]
        inner = [_dot(s[h], vh[h]) for h in heads]
        for g in range(ng):
            o_ref[0, 0, :, groups[g]] = cross[g] + inner[2 * g] + inner[2 * g + 1]

    @pl.when(jnp.logical_not(fwd))
    def _():
        for g in range(ng):
            o_ref[0, 0, :, groups[g]] = cross[g]


def _retention(p_ret, cos_t, sin_t, lg):
    B, S, _ = p_ret.shape
    tm = SEQ_TILE
    n = S // tm
    pos = lambda d, i: jnp.where(d == 0, i, n - 1 - i)
    col = lambda j: pl.BlockSpec((1, tm, WIDTH), lambda bb, d, i: (bb, pos(d, i), j))
    tab = pl.BlockSpec((tm, LANES), lambda bb, d, i: (pos(d, i), 0))
    return pl.pallas_call(
        _retention_kernel,
        name="retention",
        grid=(B, 2, n),
        in_specs=[col(0), col(1), col(2), tab, tab,
                  pl.BlockSpec((N_HEADS, LANES), lambda bb, d, i: (0, 0))],
        out_specs=pl.BlockSpec((1, 1, tm, WIDTH), lambda bb, d, i: (d, bb, pos(d, i), 0)),
        out_shape=jax.ShapeDtypeStruct((2, B, S, WIDTH), F32),
        scratch_shapes=[pltpu.VMEM((WIDTH // LANES, LANES, LANES), F32),
                        pltpu.VMEM((tm, WIDTH), F32), pltpu.VMEM((tm, WIDTH), F32),
                        pltpu.VMEM((N_HEADS, tm, tm), F32)],
        compiler_params=_cparams(("parallel", "parallel", "arbitrary")),
    )(p_ret, p_ret, p_ret, cos_t, sin_t, lg)


def _head_norm(o, bd, eps):
    mu = _segsum(o, bd) * (1.0 / HEAD_DIM)
    dlt = o - mu
    var = _segsum(dlt * dlt, bd) * (1.0 / HEAD_DIM)
    return dlt * lax.rsqrt(var + eps)


def _merge_kernel(orw_ref, bonus_ref, g_ref, oret_ref, qg_ref, gate_ref, x_ref,
                  lng_ref, lnb_ref, rlng_ref, bd_ref, pa_ref, pb_ref, wo_ref, n2_ref, wq_ref, sk_ref,
                  x1_ref, h2_ref, sc_ref):
    bd = bd_ref[...]
    o = _head_norm(orw_ref[0] + orw_ref[1], bd, RWKV_GN_EPS) * lng_ref[...] + lnb_ref[...]
    o_a = (o + bonus_ref[...]) * g_ref[...]
    o_b = _head_norm(oret_ref[0] + oret_ref[1], bd, RET_GN_EPS) * rlng_ref[...] * jax.nn.silu(qg_ref[...])
    y_a = _dot(o_a.astype(BF16), pa_ref[...])
    y_b = _dot(o_b.astype(BF16), pb_ref[...])
    gate = gate_ref[...]
    mixed = jax.nn.sigmoid(gate[:, :D_MODEL]) * y_a + jax.nn.sigmoid(gate[:, D_MODEL:]) * y_b
    x1 = x_ref[...] + _dot(mixed.astype(BF16), wo_ref[...])
    x1_ref[...] = x1
    h2 = _rmsnorm(x1, n2_ref[...])
    hb = h2.astype(BF16)
    h2_ref[...] = hb
    q = _dot(hb, wq_ref[...]).astype(BF16)
    for j in range(2 * PEER_HEADS):
        sc_ref[j] = _dot_nt(sk_ref[j], q[:, j * 128:(j + 1) * 128])


def _merge(orw, bonus, g, oret, p_ret, p_gate, x2, lng, lnb, rlng, bd, pa, pb, wo, n2, wq, sk):
    T = x2.shape[0]
    tm = TOK_TILE
    full = lambda shape: pl.BlockSpec(shape, lambda i: (0,) * len(shape))
    tok = lambda w: pl.BlockSpec((tm, w), lambda i: (i, 0))
    two = pl.BlockSpec((2, tm, WIDTH), lambda i: (0, i, 0))
    return pl.pallas_call(
        _merge_kernel,
        name="merge",
        grid=(T // tm,),
        in_specs=[two, tok(WIDTH), tok(WIDTH), two,
                  pl.BlockSpec((tm, WIDTH), lambda i: (i, 3)),
                  tok(GATE_COLS), tok(D_MODEL),
                  full((1, WIDTH)), full((1, WIDTH)), full((1, WIDTH)), full((WIDTH, WIDTH)),
                  full((WIDTH, D_MODEL)), full((WIDTH, D_MODEL)), full((D_MODEL, D_MODEL)), full((1, D_MODEL)),
                  full((D_MODEL, 2 * PEER_HEADS * 128)), full((2 * PEER_HEADS, 128, 128))],
        out_specs=[tok(D_MODEL), tok(D_MODEL),
                   pl.BlockSpec((2 * PEER_HEADS, PEER_NKEYS, tm), lambda i: (0, 0, i))],
        out_shape=[jax.ShapeDtypeStruct((T, D_MODEL), F32),
                   jax.ShapeDtypeStruct((T, D_MODEL), BF16),
                   jax.ShapeDtypeStruct((2 * PEER_HEADS, PEER_NKEYS, T), F32)],
        compiler_params=_cparams(("parallel",)),
    )(orw, bonus, g, oret, p_ret, p_gate, x2, lng, lnb, rlng, bd, pa, pb, wo, n2, wq, sk)


def _oddeven_merge(lo, hi, r):
    step = r * 2
    if step < hi - lo:
        yield from _oddeven_merge(lo, hi, step)
        yield from _oddeven_merge(lo + r, hi, step)
        yield from [(i, i + r) for i in range(lo + r, hi - r, step)]
    else:
        yield (lo, lo + r)


def _oddeven_sort(lo, hi):
    if hi - lo >= 1:
        mid = lo + (hi - lo) // 2
        yield from _oddeven_sort(lo, mid)
        yield from _oddeven_sort(mid + 1, hi)
        yield from _oddeven_merge(lo, hi, 1)


_SORT16 = tuple(_oddeven_sort(0, PEER_TOPK - 1))


def _order(x, i, j):
    hi, lo = jnp.maximum(x[i], x[j]), jnp.minimum(x[i], x[j])
    x[i], x[j] = hi, lo


def _top16(s):
    n = PEER_TOPK
    tl = s.shape[1]
    x = [s[8 * a:8 * a + 8, :] for a in range(s.shape[0] // 8)]
    x = x + [jnp.full((8, tl), NEG_BIG, F32)] * (n - len(x))
    for i, j in _SORT16:
        _order(x, i, j)
    for shift in (4, 2, 1):
        r = [pltpu.roll(x[a], shift, 0) for a in range(n)]
        x = [jnp.maximum(x[a], r[n - 1 - a]) for a in range(n)]
        for stride in (8, 4, 2, 1):
            for a in range(n):
                if a & stride == 0:
                    _order(x, a, a + stride)
    sub = lax.broadcasted_iota(jnp.int32, (8, tl), 0)
    halves = []
    for base in (0, 8):
        v = x[base + 7]
        for a in range(6, -1, -1):
            v = jnp.where(sub == a, x[base + a], v)
        halves.append(v)
    return jnp.concatenate(halves, axis=0)


def _topk_prep_kernel(sc_ref, c_ref, e1_ref, rk_ref, e2_ref):
    s1 = sc_ref[0]
    s2 = sc_ref[1]
    v1 = _top16(s1)
    v2 = _top16(s2)
    tl = s1.shape[1]
    cands = [v1[0:1, :] + v2]
    for a in range(1, 8):
        cands.append(v1[a:a + 1, :] + v2[0:8, :])
    cands.append(v1[8:16, :] + v2[0:1, :])
    tau = _top16(jnp.concatenate(cands, axis=0))[PEER_TOPK - 1:PEER_TOPK, :]

    cut = jnp.zeros_like(s1)
    rank2 = jnp.zeros_like(s2)
    cut16 = jnp.zeros_like(v1)
    for b in range(PEER_TOPK):
        vb = v2[b:b + 1, :]
        cut = jnp.where(s1 + vb >= tau, b + 1.0, cut)
        rank2 = jnp.where(vb > s2, b + 1.0, rank2)
        cut16 = jnp.where(v1 + vb >= tau, b + 1.0, cut16)
    e1v = jnp.exp(v1 - v1[0:1, :])
    e2v = jnp.exp(v2 - v2[0:1, :])
    zacc = jnp.zeros_like(v1)
    for b in range(PEER_TOPK):
        zacc = zacc + jnp.where(cut16 > b, e1v, 0.0) * e2v[b:b + 1, :]
    z = jnp.sum(zacc, axis=0, keepdims=True)
    c_ref[0] = cut
    e1_ref[0] = jnp.exp(s1 - v1[0:1, :]) * (0.5 / z)
    rk_ref[0] = rank2.astype(BF16)
    e2_ref[0] = jnp.exp(s2 - v2[0:1, :]).astype(BF16)
    del tl


def _topk_prep(sc):
    _, nk, T = sc.shape
    tl = TOPK_TILE
    blk = pl.BlockSpec((1, nk, tl), lambda h, t: (h, 0, t))
    sf = jax.ShapeDtypeStruct((PEER_HEADS, nk, T), F32)
    sb = jax.ShapeDtypeStruct((PEER_HEADS, nk, T), BF16)
    return pl.pallas_call(
        _topk_prep_kernel,
        name="topk_prep",
        grid=(PEER_HEADS, T // tl),
        in_specs=[pl.BlockSpec((2, nk, tl), lambda h, t: (h, 0, t))],
        out_specs=[blk, blk, blk, blk],
        out_shape=[sf, sf, sb, sb],
        compiler_params=_cparams(("parallel", "parallel")),
    )(sc)


def _gelu2(x):
    return x * (1.0 + lax.erf(x * (2.0 ** -0.5)))


def _peer_kernel(h_ref, u_ref, vt_ref, c_ref, e1_ref, rk_ref, e2_ref, x1_ref, nf_ref, y_ref,
                 acc_ref, act_ref, p_ref):
    e = pl.program_id(1)
    ne = pl.num_programs(1)
    nrow = PEER_EXP_TILE // PEER_NKEYS

    @pl.when(e == 0)
    def _():
        acc_ref[...] = jnp.zeros_like(acc_ref)

    act_ref[...] = _dot_nt(u_ref[...], h_ref[...])
    base = pl.multiple_of(e * nrow, nrow)
    cuts = [c_ref[h, pl.ds(base, nrow), :] for h in range(PEER_HEADS)]
    e1s = [e1_ref[h, pl.ds(base, nrow), :] for h in range(PEER_HEADS)]
    for il in range(nrow):
        rows = slice(il * PEER_NKEYS, (il + 1) * PEER_NKEYS)
        w = None
        for h in range(PEER_HEADS):
            cut = cuts[h][il:il + 1, :].astype(BF16)
            e1 = e1s[h][il:il + 1, :].astype(BF16)
            term = jnp.where(rk_ref[h] < cut, e2_ref[h], jnp.zeros((), BF16)) * e1
            w = term if w is None else w + term
        p_ref[rows, :] = w * _gelu2(act_ref[rows, :]).astype(BF16)
    acc_ref[...] += _dot(vt_ref[...], p_ref[...])

    @pl.when(e == ne - 1)
    def _():
        y = x1_ref[...] + acc_ref[...].T
        y_ref[...] = _rmsnorm(y, nf_ref[...])


def _peer(h2, u_bf16, vt_bf16, cut, e1, rank2, e2, x1, nf):
    T = h2.shape[0]
    tT = PEER_TOK_TILE
    eT = PEER_EXP_TILE
    sel = pl.BlockSpec((PEER_HEADS, PEER_NKEYS, tT), lambda t, e: (0, 0, t))
    return pl.pallas_call(
        _peer_kernel,
        name="peer",
        grid=(T // tT, PEER_EXPERTS // eT),
        in_specs=[pl.BlockSpec((tT, D_MODEL), lambda t, e: (t, 0)),
                  pl.BlockSpec((eT, D_MODEL), lambda t, e: (e, 0)),
                  pl.BlockSpec((D_MODEL, eT), lambda t, e: (0, e)),
                  sel, sel, sel, sel,
                  pl.BlockSpec((tT, D_MODEL), lambda t, e: (t, 0)),
                  pl.BlockSpec((1, D_MODEL), lambda t, e: (0, 0))],
        out_specs=pl.BlockSpec((tT, D_MODEL), lambda t, e: (t, 0)),
        out_shape=jax.ShapeDtypeStruct((T, D_MODEL), F32),
        scratch_shapes=[pltpu.VMEM((D_MODEL, tT), F32),
                        pltpu.VMEM((eT, tT), F32),
                        pltpu.VMEM((eT, tT), BF16)],
        compiler_params=_cparams(("parallel", "arbitrary")),
    )(h2, u_bf16, vt_bf16, cut, e1, rank2, e2, x1, nf)


def _rope_tables(S):
    half = HEAD_DIM // 2
    inv = ROPE_BASE ** (-jnp.arange(half, dtype=F32) / half)
    ang = jnp.arange(S, dtype=F32)[:, None] * inv[None, :]
    c = jnp.cos(ang)
    s = jnp.sin(ang)
    reps = LANES // HEAD_DIM
    cos_t = jnp.tile(jnp.concatenate([c, c], axis=1), (1, reps))
    sin_t = jnp.tile(jnp.concatenate([-s, s], axis=1), (1, reps))
    return cos_t, sin_t


def _trunk(x, w):
    B, S, _ = x.shape
    T = B * S
    x2 = x.reshape(T, D_MODEL)
    p_rwkv, p_ret, p_gate = _inproj(x2, w["norm1_g"], w["w_in"])
    r, v, kap, lw, kd, b, bonus, g = _rwkv_prep(
        p_rwkv.reshape(B, S, RWKV_COLS), w["mu"], w["w0"], w["w2"], w["a0"], w["a2"], w["g2"],
        w["k_k"], w["k_a"], w["r_k"], w["bd"])
    orw = _rwkv_scan(r, v, kap, lw, kd, b)
    cos_t, sin_t = _rope_tables(S)
    oret = _retention(p_ret.reshape(B, S, RET_COLS), cos_t, sin_t, w["lg"])
    x1, h2, sc = _merge(orw.reshape(2, T, WIDTH), bonus.reshape(T, WIDTH), g.reshape(T, WIDTH),
                        oret.reshape(2, T, WIDTH), p_ret, p_gate, x2,
                        w["ln_g"], w["ln_b"], w["ret_ln_g"], w["bd"], w["proj_a"], w["proj_b"], w["w_out"],
                        w["norm2_g"], w["wq"], w["sk"])
    cut, e1, rank2, e2 = _topk_prep(sc)
    y = _peer(h2, w["u"], w["vt"], cut, e1, rank2, e2, x1, w["normf_g"])
    return y.reshape(B, S, D_MODEL)


def kernel(x_prompt, x_sample, norm1_g, w_in, rwkv_mu, rwkv_w0, rwkv_w2, rwkv_a0, rwkv_a2, rwkv_g2, rwkv_k_k,
           rwkv_k_a, rwkv_r_k, rwkv_ln_g, rwkv_ln_b, ret_ln_g, proj_a, proj_b, w_out, norm2_g, peer_wq,
           peer_subkeys, peer_u, peer_v, normf_g):
    assert norm1_g.shape[0] == 1, "single-layer trunk"
    head_id = jnp.arange(WIDTH) // HEAD_DIM
    w = {
        "norm1_g": norm1_g[0][None, :],
        "w_in": w_in[0].astype(BF16),
        "mu": rwkv_mu[0],
        "w0": rwkv_w0[0],
        "w2": rwkv_w2[0].astype(BF16),
        "a0": rwkv_a0[0],
        "a2": rwkv_a2[0].astype(BF16),
        "g2": rwkv_g2[0].astype(BF16),
        "k_k": rwkv_k_k[0][None, :],
        "k_a": rwkv_k_a[0][None, :],
        "r_k": rwkv_r_k[0].reshape(1, WIDTH),
        "ln_g": rwkv_ln_g[0][None, :],
        "ln_b": rwkv_ln_b[0][None, :],
        "ret_ln_g": ret_ln_g[0][None, :],
        "bd": (head_id[:, None] == head_id[None, :]).astype(BF16),
        "lg": jnp.broadcast_to(jnp.log1p(-jnp.exp2(-5.0 - jnp.arange(N_HEADS, dtype=F32)))[:, None],
                               (N_HEADS, LANES)),
        "proj_a": proj_a[0].astype(BF16),
        "proj_b": proj_b[0].astype(BF16),
        "w_out": w_out[0].astype(BF16),
        "norm2_g": norm2_g[0][None, :],
        "wq": peer_wq[0].astype(BF16),
        "sk": peer_subkeys[0].reshape(2 * PEER_HEADS, PEER_NKEYS, 128).astype(BF16),
        "u": peer_u[0].astype(BF16),
        "vt": peer_v[0].T.astype(BF16),
        "normf_g": normf_g[None, :],
    }
    return (_trunk(x_prompt, w), _trunk(x_sample, w))
```

```python
import functools

import jax
import jax.numpy as jnp
from jax import lax
from jax.experimental import pallas as pl
from jax.experimental.pallas import tpu as pltpu

F32 = jnp.float32
BF16 = jnp.bfloat16

D_MODEL = 1024
HEAD_DIM = 64
N_HEADS = 8
WIDTH = N_HEADS * HEAD_DIM
RWKV_COLS = 1920
RET_COLS = 2048
GATE_COLS = 2048
RWKV_GN_EPS = 64e-5
RET_GN_EPS = 1e-6
NORM_EPS = 1e-6
ROPE_BASE = 10000.0
PEER_HEADS = 8
PEER_NKEYS = 128
PEER_TOPK = 16
PEER_EXPERTS = PEER_NKEYS * PEER_NKEYS

LANES = 128
CHUNK = 64
SEQ_TILE = 256
SCAN_GROUPS = 4
TOK_TILE = 256
PEER_TOK_TILE = 512
PEER_EXP_TILE = 2048
TOPK_TILE = 512
VMEM_LIMIT = 56 * 1024 * 1024
NEG_BIG = -1e30

_NT = (((1,), (1,)), ((), ()))


def _cparams(sem):
    return pltpu.CompilerParams(dimension_semantics=sem, vmem_limit_bytes=VMEM_LIMIT)


def _dot(a, b):
    return jnp.dot(a, b, preferred_element_type=F32)


def _dot_nt(a, b):
    return lax.dot_general(a, b, _NT, preferred_element_type=F32)


def _dot_tn(a, b):
    return lax.dot_general(a, b, (((0,), (0,)), ((), ())), preferred_element_type=F32)


def _split2(x):
    hi = x.astype(BF16)
    lo = (x - hi.astype(F32)).astype(BF16)
    return hi, lo


def _ones_mm(m_bf16, x):
    x0 = x.astype(BF16)
    r1 = x - x0.astype(F32)
    x1 = r1.astype(BF16)
    x2 = (r1 - x1.astype(F32)).astype(BF16)
    return _dot(m_bf16, x0) + (_dot(m_bf16, x1) + _dot(m_bf16, x2))


def _segsum(x, bd):
    hi, lo = _split2(x)
    return _dot(hi, bd) + _dot(lo, bd)


def _rmsnorm(x, g):
    return x * lax.rsqrt(jnp.mean(x * x, axis=-1, keepdims=True) + NORM_EPS) * g


def _inproj_kernel(x_ref, g_ref, w_ref, o1_ref, o2_ref, o3_ref):
    hb = _rmsnorm(x_ref[...], g_ref[...]).astype(BF16)
    o1_ref[...] = _dot(hb, w_ref[:, 0:RWKV_COLS])
    o2_ref[...] = _dot(hb, w_ref[:, RWKV_COLS:RWKV_COLS + RET_COLS])
    o3_ref[...] = _dot(hb, w_ref[:, RWKV_COLS + RET_COLS:])


def _inproj(x2, g, w_bf16):
    T = x2.shape[0]
    tm = TOK_TILE
    ncol = w_bf16.shape[1]
    return pl.pallas_call(
        _inproj_kernel,
        name="inproj",
        grid=(T // tm,),
        in_specs=[
            pl.BlockSpec((tm, D_MODEL), lambda i: (i, 0)),
            pl.BlockSpec((1, D_MODEL), lambda i: (0, 0)),
            pl.BlockSpec((D_MODEL, ncol), lambda i: (0, 0)),
        ],
        out_specs=[
            pl.BlockSpec((tm, RWKV_COLS), lambda i: (i, 0)),
            pl.BlockSpec((tm, RET_COLS), lambda i: (i, 0)),
            pl.BlockSpec((tm, GATE_COLS), lambda i: (i, 0)),
        ],
        out_shape=[
            jax.ShapeDtypeStruct((T, RWKV_COLS), F32),
            jax.ShapeDtypeStruct((T, RET_COLS), F32),
            jax.ShapeDtypeStruct((T, GATE_COLS), F32),
        ],
        compiler_params=_cparams(("parallel",)),
    )(x2, g, w_bf16)


def _softplus(y):
    return jnp.maximum(y, 0.0) + jnp.log1p(jnp.exp(-jnp.abs(y)))


def _rwkv_prep_kernel(p_ref, pprev_ref, pnext_ref, mu_ref, w0_ref, w2_ref, a0_ref, a2_ref, g2_ref,
                      kk_ref, ka_ref, rk_ref, bd_ref,
                      r_ref, v_ref, kap_ref, lw_ref, kd_ref, b_ref, bonus_ref, g_ref):
    i = pl.program_id(1)
    n = pl.num_programs(1)
    p = p_ref[0]
    tm = p.shape[0]
    prev_row = jnp.where(i > 0, pprev_ref[0, 7:8, :], 0.0)
    next_row = jnp.where(i < n - 1, pnext_ref[0, 0:1, :], 0.0)
    row = lax.broadcasted_iota(jnp.int32, p.shape, 0)
    prev = jnp.where(row == 0, prev_row, pltpu.roll(p, 1, 0))
    nxt = jnp.where(row == tm - 1, next_row, pltpu.roll(p, tm - 1, 0))
    ps = p + mu_ref[0:1, :] * (prev - p) + mu_ref[1:2, :] * (nxt - p)

    bd = bd_ref[...]
    r = ps[:, 0:WIDTH]
    k = ps[:, WIDTH:2 * WIDTH]
    v = ps[:, 2 * WIDTH:3 * WIDTH]
    kk = k * kk_ref[...]
    kappa = kk / jnp.maximum(jnp.sqrt(_segsum(kk * kk, bd)), 1e-12)
    r_ref[0] = r
    v_ref[0] = v
    kap_ref[0] = kappa
    ksum = jnp.zeros_like(k)
    for d in range(2):
        wl = ps[:, 1536 + 64 * d:1600 + 64 * d]
        al = ps[:, 1664 + 64 * d:1728 + 64 * d]
        z = w0_ref[d:d + 1, :] + _dot(jnp.tanh(wl).astype(BF16), w2_ref[d])
        w_log = -_softplus(-z) - 0.5
        lw_ref[d, 0] = -jnp.exp(w_log)
        a = jax.nn.sigmoid(a0_ref[d:d + 1, :] + _dot(al.astype(BF16), a2_ref[d]))
        kd = k * (1.0 + (a - 1.0) * ka_ref[...])
        kd_ref[d, 0] = kd
        b_ref[d, 0] = kappa * a
        ksum = ksum + kd
    bonus_ref[0] = _segsum(r * (0.5 * ksum) * rk_ref[...], bd) * v
    g_ref[0] = _dot(jax.nn.sigmoid(ps[:, 1792:1920]).astype(BF16), g2_ref[...])


def _rwkv_prep(p_rwkv, mu, w0, w2, a0, a2, g2, k_k, k_a, r_k, bd):
    B, S, _ = p_rwkv.shape
    tm = SEQ_TILE
    n = S // tm
    nb8 = S // 8
    full = lambda shape: pl.BlockSpec(shape, lambda b, i: (0,) * len(shape))
    tok = pl.BlockSpec((1, tm, WIDTH), lambda b, i: (b, i, 0))
    tok2 = pl.BlockSpec((2, 1, tm, WIDTH), lambda b, i: (0, b, i, 0))
    s1 = jax.ShapeDtypeStruct((B, S, WIDTH), F32)
    s2 = jax.ShapeDtypeStruct((2, B, S, WIDTH), F32)
    return pl.pallas_call(
        _rwkv_prep_kernel,
        name="rwkv_prep",
        grid=(B, n),
        in_specs=[
            pl.BlockSpec((1, tm, RWKV_COLS), lambda b, i: (b, i, 0)),
            pl.BlockSpec((1, 8, RWKV_COLS), lambda b, i: (b, jnp.maximum(i * (tm // 8) - 1, 0), 0)),
            pl.BlockSpec((1, 8, RWKV_COLS), lambda b, i: (b, jnp.minimum((i + 1) * (tm // 8), nb8 - 1), 0)),
            full((2, RWKV_COLS)), full((2, WIDTH)), full((2, 64, WIDTH)), full((2, WIDTH)),
            full((2, 64, WIDTH)), full((128, WIDTH)), full((1, WIDTH)), full((1, WIDTH)), full((1, WIDTH)),
            full((WIDTH, WIDTH)),
        ],
        out_specs=[tok, tok, tok, tok2, tok2, tok2, tok, tok],
        out_shape=[s1, s1, s1, s2, s2, s2, s1, s1],
        compiler_params=_cparams(("parallel", "parallel")),
    )(p_rwkv, p_rwkv, p_rwkv, mu, w0, w2, a0, a2, g2, k_k, k_a, r_k, bd)


def _scan_groups(groups, states, fwd, strict, incl, cum_b):
    tm = SEQ_TILE
    nck = tm // CHUNK
    ng = len(groups)
    lane = lax.broadcasted_iota(jnp.int32, (1, LANES), 1)
    gs, RRb, Ath, Rth, Vhb = [], [], [], [], []
    for (R, V, KAP, LW, KD, BV) in groups:
        g = _ones_mm(cum_b, LW)
        eng = jnp.exp(-g)
        At = -KAP * jnp.exp(g - LW)
        Rt = R * jnp.exp(g)
        gs.append(g)
        RRb.append(jnp.concatenate([BV * eng, KD * eng], axis=0).astype(BF16))
        for h in range(2):
            mh = jnp.where((lane // HEAD_DIM) == h, 1.0, 0.0)
            Ath.append(At * mh)
            Rth.append(Rt * mh)
            Vhb.append((V * mh).astype(BF16))
    chains = range(2 * ng)
    Q = [_dot_nt(jnp.concatenate([Ath[c], Rth[c]], axis=0).astype(BF16), RRb[c // 2]) for c in chains]
    Nb = [jnp.where(strict, Q[c][:tm, :tm], 0.0).astype(BF16) for c in chains]
    Aak = [jnp.where(strict, Q[c][:tm, tm:], 0.0).astype(BF16) for c in chains]
    Arb = [jnp.where(incl, Q[c][tm:, :tm], 0.0).astype(BF16) for c in chains]
    Ark = [jnp.where(incl, Q[c][tm:, tm:], 0.0).astype(BF16) for c in chains]
    X = [jnp.concatenate([Ath[c], _dot(Aak[c], Vhb[c])], axis=1) for c in chains]
    X = [X[c] + _dot(Nb[c], X[c].astype(BF16)) for c in chains]
    Pb = Nb
    for _ in range(5):
        Pb = [_dot(Pb[c], Pb[c]).astype(BF16) for c in chains]
        X = [X[c] + _dot(Pb[c], X[c].astype(BF16)) for c in chains]
    Rhh = [Rth[c] + _dot(Arb[c], X[c][:, :LANES].astype(BF16)) for c in chains]
    Ohh = [_dot(Arb[c], X[c][:, LANES:].astype(BF16)) + _dot(Ark[c], Vhb[c]) for c in chains]

    r128 = lax.broadcasted_iota(jnp.int32, (LANES, LANES), 0)
    c128 = lax.broadcasted_iota(jnp.int32, (LANES, LANES), 1)
    bdm = (r128 // HEAD_DIM) == (c128 // HEAD_DIM)
    eye = r128 == c128
    colc = lax.broadcasted_iota(jnp.int32, (1, tm), 1) // CHUNK
    Rh, Oh, Gs, Hs = [], [], [], []
    for gi, (R, V, KAP, LW, KD, BV) in enumerate(groups):
        g = gs[gi]
        Wtb = (X[2 * gi][:, :LANES] + X[2 * gi + 1][:, :LANES]).astype(BF16)
        Utb = (X[2 * gi][:, LANES:] + X[2 * gi + 1][:, LANES:]).astype(BF16)
        Rh.append(Rhh[2 * gi] + Rhh[2 * gi + 1])
        Oh.append(Ohh[2 * gi] + Ohh[2 * gi + 1])
        ends = []
        for c in range(nck):
            e_f = g[c * CHUNK + CHUNK - 1:c * CHUNK + CHUNK, :]
            e_b = g[c * CHUNK:c * CHUNK + 1, :]
            ends.append(jnp.where(fwd, e_f, e_b))
        gend = jnp.concatenate([jnp.broadcast_to(e, (CHUNK, LANES)) for e in ends], axis=0)
        egc = jnp.exp(gend - g)
        BhT = (BV * egc).T.astype(BF16)
        KhT = (KD * egc).T.astype(BF16)
        Vb = V.astype(BF16)
        Gg, Hg = [], []
        for c in range(nck):
            cm = colc == c
            Bc = jnp.where(cm, BhT, jnp.zeros((), BF16))
            Kc = jnp.where(cm, KhT, jnp.zeros((), BF16))
            Gg.append(jnp.where(eye, jnp.exp(ends[c]), 0.0) + jnp.where(bdm, _dot(Bc, Wtb), 0.0))
            Hg.append(jnp.where(bdm, _dot(Bc, Utb) + _dot(Kc, Vb), 0.0))
        Gs.append(Gg)
        Hs.append(Hg)

    St = list(states)
    outs = [[None] * nck for _ in range(ng)]
    for step in range(nck):
        c_f = step
        c_b = nck - 1 - step
        rows_f = slice(c_f * CHUNK, (c_f + 1) * CHUNK)
        rows_b = slice(c_b * CHUNK, (c_b + 1) * CHUNK)
        for gi in range(ng):
            Rc = jnp.where(fwd, Rh[gi][rows_f], Rh[gi][rows_b])
            Oc = jnp.where(fwd, Oh[gi][rows_f], Oh[gi][rows_b])
            Gc = jnp.where(fwd, Gs[gi][c_f], Gs[gi][c_b])
            Hc = jnp.where(fwd, Hs[gi][c_f], Hs[gi][c_b])
            Stb = St[gi].astype(BF16)
            outs[gi][step] = _dot(Rc.astype(BF16), Stb) + Oc
            St[gi] = _dot(Gc.astype(BF16), Stb) + Hc
    res = []
    for gi in range(ng):
        o_f = jnp.concatenate(outs[gi], axis=0)
        o_b = jnp.concatenate(outs[gi][::-1], axis=0)
        res.append(jnp.where(fwd, o_f, o_b))
    return res, St


def _rwkv_scan_kernel(r_ref, v_ref, kap_ref, lw_ref, kd_ref, b_ref, o_ref, st_ref):
    d = pl.program_id(2)
    i = pl.program_id(3)
    tm = SEQ_TILE

    @pl.when(i == 0)
    def _():
        st_ref[...] = jnp.zeros_like(st_ref)

    fwd = d == 0
    ri = lax.broadcasted_iota(jnp.int32, (tm, tm), 0)
    ci = lax.broadcasted_iota(jnp.int32, (tm, tm), 1)
    same = (ri // CHUNK) == (ci // CHUNK)
    ahead = jnp.where(fwd, ri - ci, ci - ri)
    strict = same & (ahead > 0)
    incl = same & (ahead >= 0)
    cum_b = jnp.where(incl, 1.0, 0.0).astype(BF16)
    lanes = [slice(grp * LANES, (grp + 1) * LANES) for grp in range(SCAN_GROUPS)]
    groups = [(r_ref[0, :, ls], v_ref[0, :, ls], kap_ref[0, :, ls], lw_ref[0, 0, :, ls],
               kd_ref[0, 0, :, ls], b_ref[0, 0, :, ls]) for ls in lanes]
    outs, sts = _scan_groups(groups, [st_ref[grp] for grp in range(SCAN_GROUPS)], fwd, strict, incl, cum_b)
    for grp, ls in enumerate(lanes):
        o_ref[0, 0, :, ls] = outs[grp]
        st_ref[grp] = sts[grp]


def _rwkv_scan(r, v, kap, lw, kd, b):
    B, S, _ = r.shape
    tm = SEQ_TILE
    n = S // tm
    wb = SCAN_GROUPS * LANES
    pos = lambda d, i: jnp.where(d == 0, i, n - 1 - i)
    tok = pl.BlockSpec((1, tm, wb), lambda bb, hp, d, i: (bb, pos(d, i), hp))
    tok2 = pl.BlockSpec((1, 1, tm, wb), lambda bb, hp, d, i: (d, bb, pos(d, i), hp))
    return pl.pallas_call(
        _rwkv_scan_kernel,
        name="rwkv_scan",
        grid=(B, WIDTH // wb, 2, n),
        in_specs=[tok, tok, tok, tok2, tok2, tok2],
        out_specs=tok2,
        out_shape=jax.ShapeDtypeStruct((2, B, S, WIDTH), F32),
        scratch_shapes=[pltpu.VMEM((SCAN_GROUPS, LANES, LANES), F32)],
        compiler_params=_cparams(("parallel", "parallel", "parallel", "arbitrary")),
    )(r, v, kap, lw, kd, b)


def _retention_kernel(q_ref, k_ref, v_ref, cos_ref, sin_ref, lg_ref, o_ref, st_ref, xi_ref, zt_ref, dm_ref):
    d = pl.program_id(1)
    i = pl.program_id(2)
    tm = SEQ_TILE
    ng = WIDTH // LANES
    groups = [slice(g * LANES, (g + 1) * LANES) for g in range(ng)]
    fwd = d == 0
    lane = lax.broadcasted_iota(jnp.int32, (1, LANES), 1)
    lgl = [jnp.where(lane < HEAD_DIM, lg_ref[2 * g:2 * g + 1, :], lg_ref[2 * g + 1:2 * g + 2, :])
           for g in range(ng)]

    @pl.when(i == 0)
    def _():
        st_ref[...] = jnp.zeros_like(st_ref)
        rowf = lax.broadcasted_iota(jnp.int32, (tm, LANES), 0).astype(F32)
        e_q = jnp.where(fwd, rowf + 1.0, tm - rowf)
        e_k = jnp.where(fwd, tm - 1.0 - rowf, rowf)
        for g in range(ng):
            xi_ref[:, groups[g]] = jnp.exp(e_q * lgl[g])
            zt_ref[:, groups[g]] = jnp.exp(e_k * lgl[g])

    @pl.when(jnp.logical_and(i == 0, fwd))
    def _():
        ri = lax.broadcasted_iota(jnp.int32, (tm, tm), 0)
        ci = lax.broadcasted_iota(jnp.int32, (tm, tm), 1)
        dist = jnp.abs(ri - ci).astype(F32)
        for h in range(N_HEADS):
            dm_ref[h] = jnp.exp(dist * lg_ref[h:h + 1, 0:1])

    cosv = jnp.concatenate([cos_ref[...]] * ng, axis=1)
    sinv = jnp.concatenate([sin_ref[...]] * ng, axis=1)
    half = HEAD_DIM // 2
    lane_w = lax.broadcasted_iota(jnp.int32, (1, WIDTH), 1)
    first = (lane_w % HEAD_DIM) < half

    def rot(t):
        sw = jnp.where(first, pltpu.roll(t, WIDTH - half, 1), pltpu.roll(t, half, 1))
        return t * cosv + sw * sinv

    q = rot(q_ref[0]) * (HEAD_DIM ** -0.5)
    k = rot(k_ref[0])
    v = v_ref[0]
    qx = (q * xi_ref[...]).astype(BF16)
    kz = (k * zt_ref[...]).astype(BF16)
    vb = v.astype(BF16)

    r128 = lax.broadcasted_iota(jnp.int32, (LANES, LANES), 0)
    c128 = lax.broadcasted_iota(jnp.int32, (LANES, LANES), 1)
    bdm = (r128 // HEAD_DIM) == (c128 // HEAD_DIM)
    St = [st_ref[g] for g in range(ng)]
    cross = [_dot(qx[:, groups[g]], St[g].astype(BF16)) for g in range(ng)]
    kv = [_dot_tn(kz[:, groups[g]], vb[:, groups[g]]) for g in range(ng)]
    for g in range(ng):
        st_ref[g] = jnp.exp(tm * lgl[g]) * St[g] + jnp.where(bdm, kv[g], 0.0)

    @pl.when(fwd)
    def _():
        heads = range(N_HEADS)
        mh = [jnp.where((lane // HEAD_DIM) == h % 2, 1.0, 0.0) for h in heads]
        qh = [(q[:, groups[h // 2]] * mh[h]).astype(BF16) for h in heads]
        vh = [(v[:, groups[h // 2]] * mh[h]).astype(BF16) for h in heads]
        kb = [k[:, groups[g]].astype(BF16) for g in range(ng)]
        s = [_dot_nt(qh[h], kb[h // 2]) for h in heads]
        s = [(s[h] * dm_ref[h]).astype(BF16) for h in heads]
        inner = [_dot(s[h], vh[h]) for h in heads]
        for g in range(ng):
            o_ref[0, 0, :, groups[g]] = cross[g] + inner[2 * g] + inner[2 * g + 1]

    @pl.when(jnp.logical_not(fwd))
    def _():
        for g in range(ng):
            o_ref[0, 0, :, groups[g]] = cross[g]


def _retention(p_ret, cos_t, sin_t, lg):
    B, S, _ = p_ret.shape
    tm = SEQ_TILE
    n = S // tm
    pos = lambda d, i: jnp.where(d == 0, i, n - 1 - i)
    col = lambda j: pl.BlockSpec((1, tm, WIDTH), lambda bb, d, i: (bb, pos(d, i), j))
    tab = pl.BlockSpec((tm, LANES), lambda bb, d, i: (pos(d, i), 0))
    return pl.pallas_call(
        _retention_kernel,
        name="retention",
        grid=(B, 2, n),
        in_specs=[col(0), col(1), col(2), tab, tab,
                  pl.BlockSpec((N_HEADS, LANES), lambda bb, d, i: (0, 0))],
        out_specs=pl.BlockSpec((1, 1, tm, WIDTH), lambda bb, d, i: (d, bb, pos(d, i), 0)),
        out_shape=jax.ShapeDtypeStruct((2, B, S, WIDTH), F32),
        scratch_shapes=[pltpu.VMEM((WIDTH // LANES, LANES, LANES), F32),
                        pltpu.VMEM((tm, WIDTH), F32), pltpu.VMEM((tm, WIDTH), F32),
                        pltpu.VMEM((N_HEADS, tm, tm), F32)],
        compiler_params=_cparams(("parallel", "parallel", "arbitrary")),
    )(p_ret, p_ret, p_ret, cos_t, sin_t, lg)


def _head_norm(o, bd, eps):
    mu = _segsum(o, bd) * (1.0 / HEAD_DIM)
    dlt = o - mu
    var = _segsum(dlt * dlt, bd) * (1.0 / HEAD_DIM)
    return dlt * lax.rsqrt(var + eps)


def _merge_kernel(orw_ref, bonus_ref, g_ref, oret_ref, qg_ref, gate_ref, x_ref,
                  lng_ref, lnb_ref, rlng_ref, bd_ref, pa_ref, pb_ref, wo_ref, n2_ref, wq_ref, sk_ref,
                  x1_ref, h2_ref, sc_ref):
    bd = bd_ref[...]
    o = _head_norm(orw_ref[0] + orw_ref[1], bd, RWKV_GN_EPS) * lng_ref[...] + lnb_ref[...]
    o_a = (o + bonus_ref[...]) * g_ref[...]
    o_b = _head_norm(oret_ref[0] + oret_ref[1], bd, RET_GN_EPS) * rlng_ref[...] * jax.nn.silu(qg_ref[...])
    y_a = _dot(o_a.astype(BF16), pa_ref[...])
    y_b = _dot(o_b.astype(BF16), pb_ref[...])
    gate = gate_ref[...]
    mixed = jax.nn.sigmoid(gate[:, :D_MODEL]) * y_a + jax.nn.sigmoid(gate[:, D_MODEL:]) * y_b
    x1 = x_ref[...] + _dot(mixed.astype(BF16), wo_ref[...])
    x1_ref[...] = x1
    h2 = _rmsnorm(x1, n2_ref[...])
    hb = h2.astype(BF16)
    h2_ref[...] = hb
    q = _dot(hb, wq_ref[...]).astype(BF16)
    for j in range(2 * PEER_HEADS):
        sc_ref[j] = _dot_nt(sk_ref[j], q[:, j * 128:(j + 1) * 128])


def _merge(orw, bonus, g, oret, p_ret, p_gate, x2, lng, lnb, rlng, bd, pa, pb, wo, n2, wq, sk):
    T = x2.shape[0]
    tm = TOK_TILE
    full = lambda shape: pl.BlockSpec(shape, lambda i: (0,) * len(shape))
    tok = lambda w: pl.BlockSpec((tm, w), lambda i: (i, 0))
    two = pl.BlockSpec((2, tm, WIDTH), lambda i: (0, i, 0))
    return pl.pallas_call(
        _merge_kernel,
        name="merge",
        grid=(T // tm,),
        in_specs=[two, tok(WIDTH), tok(WIDTH), two,
                  pl.BlockSpec((tm, WIDTH), lambda i: (i, 3)),
                  tok(GATE_COLS), tok(D_MODEL),
                  full((1, WIDTH)), full((1, WIDTH)), full((1, WIDTH)), full((WIDTH, WIDTH)),
                  full((WIDTH, D_MODEL)), full((WIDTH, D_MODEL)), full((D_MODEL, D_MODEL)), full((1, D_MODEL)),
                  full((D_MODEL, 2 * PEER_HEADS * 128)), full((2 * PEER_HEADS, 128, 128))],
        out_specs=[tok(D_MODEL), tok(D_MODEL),
                   pl.BlockSpec((2 * PEER_HEADS, PEER_NKEYS, tm), lambda i: (0, 0, i))],
        out_shape=[jax.ShapeDtypeStruct((T, D_MODEL), F32),
                   jax.ShapeDtypeStruct((T, D_MODEL), BF16),
                   jax.ShapeDtypeStruct((2 * PEER_HEADS, PEER_NKEYS, T), F32)],
        compiler_params=_cparams(("parallel",)),
    )(orw, bonus, g, oret, p_ret, p_gate, x2, lng, lnb, rlng, bd, pa, pb, wo, n2, wq, sk)


def _oddeven_merge(lo, hi, r):
    step = r * 2
    if step < hi - lo:
        yield from _oddeven_merge(lo, hi, step)
        yield from _oddeven_merge(lo + r, hi, step)
        yield from [(i, i + r) for i in range(lo + r, hi - r, step)]
    else:
        yield (lo, lo + r)


def _oddeven_sort(lo, hi):
    if hi - lo >= 1:
        mid = lo + (hi - lo) // 2
        yield from _oddeven_sort(lo, mid)
        yield from _oddeven_sort(mid + 1, hi)
        yield from _oddeven_merge(lo, hi, 1)


_SORT16 = tuple(_oddeven_sort(0, PEER_TOPK - 1))


def _order(x, i, j):
    hi, lo = jnp.maximum(x[i], x[j]), jnp.minimum(x[i], x[j])
    x[i], x[j] = hi, lo


def _top16(s):
    n = PEER_TOPK
    tl = s.shape[1]
    x = [s[8 * a:8 * a + 8, :] for a in range(s.shape[0] // 8)]
    x = x + [jnp.full((8, tl), NEG_BIG, F32)] * (n - len(x))
    for i, j in _SORT16:
        _order(x, i, j)
    for shift in (4, 2, 1):
        r = [pltpu.roll(x[a], shift, 0) for a in range(n)]
        x = [jnp.maximum(x[a], r[n - 1 - a]) for a in range(n)]
        for stride in (8, 4, 2, 1):
            for a in range(n):
                if a & stride == 0:
                    _order(x, a, a + stride)
    sub = lax.broadcasted_iota(jnp.int32, (8, tl), 0)
    halves = []
    for base in (0, 8):
        v = x[base + 7]
        for a in range(6, -1, -1):
            v = jnp.where(sub == a, x[base + a], v)
        halves.append(v)
    return jnp.concatenate(halves, axis=0)


def _topk_prep_kernel(sc_ref, c_ref, e1_ref, rk_ref, e2_ref):
    s1 = sc_ref[0]
    s2 = sc_ref[1]
    v1 = _top16(s1)
    v2 = _top16(s2)
    tl = s1.shape[1]
    cands = [v1[0:1, :] + v2]
    for a in range(1, 8):
        cands.append(v1[a:a + 1, :] + v2[0:8, :])
    cands.append(v1[8:16, :] + v2[0:1, :])
    tau = _top16(jnp.concatenate(cands, axis=0))[PEER_TOPK - 1:PEER_TOPK, :]

    cut = jnp.zeros_like(s1)
    rank2 = jnp.zeros_like(s2)
    cut16 = jnp.zeros_like(v1)
    for b in range(PEER_TOPK):
        vb = v2[b:b + 1, :]
        cut = jnp.where(s1 + vb >= tau, b + 1.0, cut)
        rank2 = jnp.where(vb > s2, b + 1.0, rank2)
        cut16 = jnp.where(v1 + vb >= tau, b + 1.0, cut16)
    e1v = jnp.exp(v1 - v1[0:1, :])
    e2v = jnp.exp(v2 - v2[0:1, :])
    zacc = jnp.zeros_like(v1)
    for b in range(PEER_TOPK):
        zacc = zacc + jnp.where(cut16 > b, e1v, 0.0) * e2v[b:b + 1, :]
    z = jnp.sum(zacc, axis=0, keepdims=True)
    c_ref[0] = cut
    e1_ref[0] = jnp.exp(s1 - v1[0:1, :]) * (0.5 / z)
    rk_ref[0] = rank2.astype(BF16)
    e2_ref[0] = jnp.exp(s2 - v2[0:1, :]).astype(BF16)
    del tl


def _topk_prep(sc):
    _, nk, T = sc.shape
    tl = TOPK_TILE
    blk = pl.BlockSpec((1, nk, tl), lambda h, t: (h, 0, t))
    sf = jax.ShapeDtypeStruct((PEER_HEADS, nk, T), F32)
    sb = jax.ShapeDtypeStruct((PEER_HEADS, nk, T), BF16)
    return pl.pallas_call(
        _topk_prep_kernel,
        name="topk_prep",
        grid=(PEER_HEADS, T // tl),
        in_specs=[pl.BlockSpec((2, nk, tl), lambda h, t: (h, 0, t))],
        out_specs=[blk, blk, blk, blk],
        out_shape=[sf, sf, sb, sb],
        compiler_params=_cparams(("parallel", "parallel")),
    )(sc)


def _gelu2(x):
    return x * (1.0 + lax.erf(x * (2.0 ** -0.5)))


def _peer_kernel(h_ref, u_ref, vt_ref, c_ref, e1_ref, rk_ref, e2_ref, x1_ref, nf_ref, y_ref,
                 acc_ref, act_ref, p_ref):
    e = pl.program_id(1)
    ne = pl.num_programs(1)
    nrow = PEER_EXP_TILE // PEER_NKEYS

    @pl.when(e == 0)
    def _():
        acc_ref[...] = jnp.zeros_like(acc_ref)

    act_ref[...] = _dot_nt(u_ref[...], h_ref[...])
    base = pl.multiple_of(e * nrow, nrow)
    cuts = [c_ref[h, pl.ds(base, nrow), :] for h in range(PEER_HEADS)]
    e1s = [e1_ref[h, pl.ds(base, nrow), :] for h in range(PEER_HEADS)]
    for il in range(nrow):
        rows = slice(il * PEER_NKEYS, (il + 1) * PEER_NKEYS)
        w = None
        for h in range(PEER_HEADS):
            cut = cuts[h][il:il + 1, :].astype(BF16)
            e1 = e1s[h][il:il + 1, :].astype(BF16)
            term = jnp.where(rk_ref[h] < cut, e2_ref[h], jnp.zeros((), BF16)) * e1
            w = term if w is None else w + term
        p_ref[rows, :] = w * _gelu2(act_ref[rows, :].astype(BF16))
    acc_ref[...] += _dot(vt_ref[...], p_ref[...])

    @pl.when(e == ne - 1)
    def _():
        y = x1_ref[...] + acc_ref[...].T
        y_ref[...] = _rmsnorm(y, nf_ref[...])


def _peer(h2, u_bf16, vt_bf16, cut, e1, rank2, e2, x1, nf):
    T = h2.shape[0]
    tT = PEER_TOK_TILE
    eT = PEER_EXP_TILE
    sel = pl.BlockSpec((PEER_HEADS, PEER_NKEYS, tT), lambda t, e: (0, 0, t))
    return pl.pallas_call(
        _peer_kernel,
        name="peer",
        grid=(T // tT, PEER_EXPERTS // eT),
        in_specs=[pl.BlockSpec((tT, D_MODEL), lambda t, e: (t, 0)),
                  pl.BlockSpec((eT, D_MODEL), lambda t, e: (e, 0)),
                  pl.BlockSpec((D_MODEL, eT), lambda t, e: (0, e)),
                  sel, sel, sel, sel,
                  pl.BlockSpec((tT, D_MODEL), lambda t, e: (t, 0)),
                  pl.BlockSpec((1, D_MODEL), lambda t, e: (0, 0))],
        out_specs=pl.BlockSpec((tT, D_MODEL), lambda t, e: (t, 0)),
        out_shape=jax.ShapeDtypeStruct((T, D_MODEL), F32),
        scratch_shapes=[pltpu.VMEM((D_MODEL, tT), F32),
                        pltpu.VMEM((eT, tT), F32),
                        pltpu.VMEM((eT, tT), BF16)],
        compiler_params=_cparams(("parallel", "arbitrary")),
    )(h2, u_bf16, vt_bf16, cut, e1, rank2, e2, x1, nf)


def _rope_tables(S):
    half = HEAD_DIM // 2
    inv = ROPE_BASE ** (-jnp.arange(half, dtype=F32) / half)
    ang = jnp.arange(S, dtype=F32)[:, None] * inv[None, :]
    c = jnp.cos(ang)
    s = jnp.sin(ang)
    reps = LANES // HEAD_DIM
    cos_t = jnp.tile(jnp.concatenate([c, c], axis=1), (1, reps))
    sin_t = jnp.tile(jnp.concatenate([-s, s], axis=1), (1, reps))
    return cos_t, sin_t


def _trunk(x, w):
    B, S, _ = x.shape
    T = B * S
    x2 = x.reshape(T, D_MODEL)
    p_rwkv, p_ret, p_gate = _inproj(x2, w["norm1_g"], w["w_in"])
    r, v, kap, lw, kd, b, bonus, g = _rwkv_prep(
        p_rwkv.reshape(B, S, RWKV_COLS), w["mu"], w["w0"], w["w2"], w["a0"], w["a2"], w["g2"],
        w["k_k"], w["k_a"], w["r_k"], w["bd"])
    orw = _rwkv_scan(r, v, kap, lw, kd, b)
    cos_t, sin_t = _rope_tables(S)
    oret = _retention(p_ret.reshape(B, S, RET_COLS), cos_t, sin_t, w["lg"])
    x1, h2, sc = _merge(orw.reshape(2, T, WIDTH), bonus.reshape(T, WIDTH), g.reshape(T, WIDTH),
                        oret.reshape(2, T, WIDTH), p_ret, p_gate, x2,
                        w["ln_g"], w["ln_b"], w["ret_ln_g"], w["bd"], w["proj_a"], w["proj_b"], w["w_out"],
                        w["norm2_g"], w["wq"], w["sk"])
    cut, e1, rank2, e2 = _topk_prep(sc)
    y = _peer(h2, w["u"], w["vt"], cut, e1, rank2, e2, x1, w["normf_g"])
    return y.reshape(B, S, D_MODEL)


def kernel(x_prompt, x_sample, norm1_g, w_in, rwkv_mu, rwkv_w0, rwkv_w2, rwkv_a0, rwkv_a2, rwkv_g2, rwkv_k_k,
           rwkv_k_a, rwkv_r_k, rwkv_ln_g, rwkv_ln_b, ret_ln_g, proj_a, proj_b, w_out, norm2_g, peer_wq,
           peer_subkeys, peer_u, peer_v, normf_g):
    assert norm1_g.shape[0] == 1, "single-layer trunk"
    head_id = jnp.arange(WIDTH) // HEAD_DIM
    w = {
        "norm1_g": norm1_g[0][None, :],
        "w_in": w_in[0].astype(BF16),
        "mu": rwkv_mu[0],
        "w0": rwkv_w0[0],
        "w2": rwkv_w2[0].astype(BF16),
        "a0": rwkv_a0[0],
        "a2": rwkv_a2[0].astype(BF16),
        "g2": rwkv_g2[0].astype(BF16),
        "k_k": rwkv_k_k[0][None, :],
        "k_a": rwkv_k_a[0][None, :],
        "r_k": rwkv_r_k[0].reshape(1, WIDTH),
        "ln_g": rwkv_ln_g[0][None, :],
        "ln_b": rwkv_ln_b[0][None, :],
        "ret_ln_g": ret_ln_g[0][None, :],
        "bd": (head_id[:, None] == head_id[None, :]).astype(BF16),
        "lg": jnp.broadcast_to(jnp.log1p(-jnp.exp2(-5.0 - jnp.arange(N_HEADS, dtype=F32)))[:, None],
                               (N_HEADS, LANES)),
        "proj_a": proj_a[0].astype(BF16),
        "proj_b": proj_b[0].astype(BF16),
        "w_out": w_out[0].astype(BF16),
        "norm2_g": norm2_g[0][None, :],
        "wq": peer_wq[0].astype(BF16),
        "sk": peer_subkeys[0].reshape(2 * PEER_HEADS, PEER_NKEYS, 128).astype(BF16),
        "u": peer_u[0].astype(BF16),
        "vt": peer_v[0].T.astype(BF16),
        "normf_g": normf_g[None, :],
    }
    return (_trunk(x_prompt, w), _trunk(x_sample, w))
```

```python
import functools

import jax
import jax.numpy as jnp
from jax import lax
from jax.experimental import pallas as pl
from jax.experimental.pallas import tpu as pltpu

F32 = jnp.float32
BF16 = jnp.bfloat16

D_MODEL = 1024
HEAD_DIM = 64
N_HEADS = 8
WIDTH = N_HEADS * HEAD_DIM
RWKV_COLS = 1920
RET_COLS = 2048
GATE_COLS = 2048
RWKV_GN_EPS = 64e-5
RET_GN_EPS = 1e-6
NORM_EPS = 1e-6
ROPE_BASE = 10000.0
PEER_HEADS = 8
PEER_NKEYS = 128
PEER_TOPK = 16
PEER_EXPERTS = PEER_NKEYS * PEER_NKEYS

LANES = 128
CHUNK = 64
SEQ_TILE = 256
SCAN_GROUPS = 4
TOK_TILE = 256
PEER_TOK_TILE = 512
PEER_EXP_TILE = 2048
TOPK_TILE = 512
VMEM_LIMIT = 56 * 1024 * 1024
NEG_BIG = -1e30

_NT = (((1,), (1,)), ((), ()))


def _cparams(sem):
    return pltpu.CompilerParams(dimension_semantics=sem, vmem_limit_bytes=VMEM_LIMIT)


def _dot(a, b):
    return jnp.dot(a, b, preferred_element_type=F32)


def _dot_nt(a, b):
    return lax.dot_general(a, b, _NT, preferred_element_type=F32)


def _dot_tn(a, b):
    return lax.dot_general(a, b, (((0,), (0,)), ((), ())), preferred_element_type=F32)


def _split2(x):
    hi = x.astype(BF16)
    lo = (x - hi.astype(F32)).astype(BF16)
    return hi, lo


def _ones_mm(m_bf16, x):
    x0 = x.astype(BF16)
    r1 = x - x0.astype(F32)
    x1 = r1.astype(BF16)
    x2 = (r1 - x1.astype(F32)).astype(BF16)
    return _dot(m_bf16, x0) + (_dot(m_bf16, x1) + _dot(m_bf16, x2))


def _segsum(x, bd):
    hi, lo = _split2(x)
    return _dot(hi, bd) + _dot(lo, bd)


def _rmsnorm(x, g):
    return x * lax.rsqrt(jnp.mean(x * x, axis=-1, keepdims=True) + NORM_EPS) * g


def _inproj_kernel(x_ref, g_ref, w_ref, o1_ref, o2_ref, o3_ref):
    hb = _rmsnorm(x_ref[...], g_ref[...]).astype(BF16)
    o1_ref[...] = _dot(hb, w_ref[:, 0:RWKV_COLS])
    o2_ref[...] = _dot(hb, w_ref[:, RWKV_COLS:RWKV_COLS + RET_COLS])
    o3_ref[...] = _dot(hb, w_ref[:, RWKV_COLS + RET_COLS:])


def _inproj(x2, g, w_bf16):
    T = x2.shape[0]
    tm = TOK_TILE
    ncol = w_bf16.shape[1]
    return pl.pallas_call(
        _inproj_kernel,
        name="inproj",
        grid=(T // tm,),
        in_specs=[
            pl.BlockSpec((tm, D_MODEL), lambda i: (i, 0)),
            pl.BlockSpec((1, D_MODEL), lambda i: (0, 0)),
            pl.BlockSpec((D_MODEL, ncol), lambda i: (0, 0)),
        ],
        out_specs=[
            pl.BlockSpec((tm, RWKV_COLS), lambda i: (i, 0)),
            pl.BlockSpec((tm, RET_COLS), lambda i: (i, 0)),
            pl.BlockSpec((tm, GATE_COLS), lambda i: (i, 0)),
        ],
        out_shape=[
            jax.ShapeDtypeStruct((T, RWKV_COLS), F32),
            jax.ShapeDtypeStruct((T, RET_COLS), F32),
            jax.ShapeDtypeStruct((T, GATE_COLS), F32),
        ],
        compiler_params=_cparams(("parallel",)),
    )(x2, g, w_bf16)


def _softplus(y):
    return jnp.maximum(y, 0.0) + jnp.log1p(jnp.exp(-jnp.abs(y)))


def _rwkv_prep_kernel(p_ref, pprev_ref, pnext_ref, mu_ref, w0_ref, w2_ref, a0_ref, a2_ref, g2_ref,
                      kk_ref, ka_ref, rk_ref, bd_ref,
                      r_ref, v_ref, kap_ref, lw_ref, kd_ref, b_ref, bonus_ref, g_ref):
    i = pl.program_id(1)
    n = pl.num_programs(1)
    p = p_ref[0]
    tm = p.shape[0]
    prev_row = jnp.where(i > 0, pprev_ref[0, 7:8, :], 0.0)
    next_row = jnp.where(i < n - 1, pnext_ref[0, 0:1, :], 0.0)
    row = lax.broadcasted_iota(jnp.int32, p.shape, 0)
    prev = jnp.where(row == 0, prev_row, pltpu.roll(p, 1, 0))
    nxt = jnp.where(row == tm - 1, next_row, pltpu.roll(p, tm - 1, 0))
    ps = p + mu_ref[0:1, :] * (prev - p) + mu_ref[1:2, :] * (nxt - p)

    bd = bd_ref[...]
    r = ps[:, 0:WIDTH]
    k = ps[:, WIDTH:2 * WIDTH]
    v = ps[:, 2 * WIDTH:3 * WIDTH]
    kk = k * kk_ref[...]
    kappa = kk / jnp.maximum(jnp.sqrt(_segsum(kk * kk, bd)), 1e-12)
    r_ref[0] = r
    v_ref[0] = v
    kap_ref[0] = kappa
    ksum = jnp.zeros_like(k)
    for d in range(2):
        wl = ps[:, 1536 + 64 * d:1600 + 64 * d]
        al = ps[:, 1664 + 64 * d:1728 + 64 * d]
        z = w0_ref[d:d + 1, :] + _dot(jnp.tanh(wl).astype(BF16), w2_ref[d])
        w_log = -_softplus(-z) - 0.5
        lw_ref[d, 0] = -jnp.exp(w_log)
        a = jax.nn.sigmoid(a0_ref[d:d + 1, :] + _dot(al.astype(BF16), a2_ref[d]))
        kd = k * (1.0 + (a - 1.0) * ka_ref[...])
        kd_ref[d, 0] = kd
        b_ref[d, 0] = kappa * a
        ksum = ksum + kd
    bonus_ref[0] = _segsum(r * (0.5 * ksum) * rk_ref[...], bd) * v
    g_ref[0] = _dot(jax.nn.sigmoid(ps[:, 1792:1920]).astype(BF16), g2_ref[...])


def _rwkv_prep(p_rwkv, mu, w0, w2, a0, a2, g2, k_k, k_a, r_k, bd):
    B, S, _ = p_rwkv.shape
    tm = SEQ_TILE
    n = S // tm
    nb8 = S // 8
    full = lambda shape: pl.BlockSpec(shape, lambda b, i: (0,) * len(shape))
    tok = pl.BlockSpec((1, tm, WIDTH), lambda b, i: (b, i, 0))
    tok2 = pl.BlockSpec((2, 1, tm, WIDTH), lambda b, i: (0, b, i, 0))
    s1 = jax.ShapeDtypeStruct((B, S, WIDTH), F32)
    s2 = jax.ShapeDtypeStruct((2, B, S, WIDTH), F32)
    return pl.pallas_call(
        _rwkv_prep_kernel,
        name="rwkv_prep",
        grid=(B, n),
        in_specs=[
            pl.BlockSpec((1, tm, RWKV_COLS), lambda b, i: (b, i, 0)),
            pl.BlockSpec((1, 8, RWKV_COLS), lambda b, i: (b, jnp.maximum(i * (tm // 8) - 1, 0), 0)),
            pl.BlockSpec((1, 8, RWKV_COLS), lambda b, i: (b, jnp.minimum((i + 1) * (tm // 8), nb8 - 1), 0)),
            full((2, RWKV_COLS)), full((2, WIDTH)), full((2, 64, WIDTH)), full((2, WIDTH)),
            full((2, 64, WIDTH)), full((128, WIDTH)), full((1, WIDTH)), full((1, WIDTH)), full((1, WIDTH)),
            full((WIDTH, WIDTH)),
        ],
        out_specs=[tok, tok, tok, tok2, tok2, tok2, tok, tok],
        out_shape=[s1, s1, s1, s2, s2, s2, s1, s1],
        compiler_params=_cparams(("parallel", "parallel")),
    )(p_rwkv, p_rwkv, p_rwkv, mu, w0, w2, a0, a2, g2, k_k, k_a, r_k, bd)


def _scan_groups(groups, states, fwd, strict, incl, cum_b):
    tm = SEQ_TILE
    nck = tm // CHUNK
    ng = len(groups)
    lane = lax.broadcasted_iota(jnp.int32, (1, LANES), 1)
    gs, RRb, Ath, Rth, Vhb = [], [], [], [], []
    for (R, V, KAP, LW, KD, BV) in groups:
        g = _ones_mm(cum_b, LW)
        eng = jnp.exp(-g)
        At = -KAP * jnp.exp(g - LW)
        Rt = R * jnp.exp(g)
        gs.append(g)
        RRb.append(jnp.concatenate([BV * eng, KD * eng], axis=0).astype(BF16))
        for h in range(2):
            mh = jnp.where((lane // HEAD_DIM) == h, 1.0, 0.0)
            Ath.append(At * mh)
            Rth.append(Rt * mh)
            Vhb.append((V * mh).astype(BF16))
    chains = range(2 * ng)
    Q = [_dot_nt(jnp.concatenate([Ath[c], Rth[c]], axis=0).astype(BF16), RRb[c // 2]) for c in chains]
    Nb = [jnp.where(strict, Q[c][:tm, :tm], 0.0).astype(BF16) for c in chains]
    Aak = [jnp.where(strict, Q[c][:tm, tm:], 0.0).astype(BF16) for c in chains]
    Arb = [jnp.where(incl, Q[c][tm:, :tm], 0.0).astype(BF16) for c in chains]
    Ark = [jnp.where(incl, Q[c][tm:, tm:], 0.0).astype(BF16) for c in chains]
    X = [jnp.concatenate([Ath[c], _dot(Aak[c], Vhb[c])], axis=1) for c in chains]
    X = [X[c] + _dot(Nb[c], X[c].astype(BF16)) for c in chains]
    Pb = Nb
    for _ in range(5):
        Pb = [_dot(Pb[c], Pb[c]).astype(BF16) for c in chains]
        X = [X[c] + _dot(Pb[c], X[c].astype(BF16)) for c in chains]
    Rhh = [Rth[c] + _dot(Arb[c], X[c][:, :LANES].astype(BF16)) for c in chains]
    Ohh = [_dot(Arb[c], X[c][:, LANES:].astype(BF16)) + _dot(Ark[c], Vhb[c]) for c in chains]

    r128 = lax.broadcasted_iota(jnp.int32, (LANES, LANES), 0)
    c128 = lax.broadcasted_iota(jnp.int32, (LANES, LANES), 1)
    bdm = (r128 // HEAD_DIM) == (c128 // HEAD_DIM)
    eye = r128 == c128
    colc = lax.broadcasted_iota(jnp.int32, (1, tm), 1) // CHUNK
    Rh, Oh, Gs, Hs = [], [], [], []
    for gi, (R, V, KAP, LW, KD, BV) in enumerate(groups):
        g = gs[gi]
        Wtb = (X[2 * gi][:, :LANES] + X[2 * gi + 1][:, :LANES]).astype(BF16)
        Utb = (X[2 * gi][:, LANES:] + X[2 * gi + 1][:, LANES:]).astype(BF16)
        Rh.append(Rhh[2 * gi] + Rhh[2 * gi + 1])
        Oh.append(Ohh[2 * gi] + Ohh[2 * gi + 1])
        ends = []
        for c in range(nck):
            e_f = g[c * CHUNK + CHUNK - 1:c * CHUNK + CHUNK, :]
            e_b = g[c * CHUNK:c * CHUNK + 1, :]
            ends.append(jnp.where(fwd, e_f, e_b))
        gend = jnp.concatenate([jnp.broadcast_to(e, (CHUNK, LANES)) for e in ends], axis=0)
        egc = jnp.exp(gend - g)
        BhT = (BV * egc).T.astype(BF16)
        KhT = (KD * egc).T.astype(BF16)
        Vb = V.astype(BF16)
        Gg, Hg = [], []
        for c in range(nck):
            cm = colc == c
            Bc = jnp.where(cm, BhT, jnp.zeros((), BF16))
            Kc = jnp.where(cm, KhT, jnp.zeros((), BF16))
            Gg.append(jnp.where(eye, jnp.exp(ends[c]), 0.0) + jnp.where(bdm, _dot(Bc, Wtb), 0.0))
            Hg.append(jnp.where(bdm, _dot(Bc, Utb) + _dot(Kc, Vb), 0.0))
        Gs.append(Gg)
        Hs.append(Hg)

    St = list(states)
    outs = [[None] * nck for _ in range(ng)]
    for step in range(nck):
        c_f = step
        c_b = nck - 1 - step
        rows_f = slice(c_f * CHUNK, (c_f + 1) * CHUNK)
        rows_b = slice(c_b * CHUNK, (c_b + 1) * CHUNK)
        for gi in range(ng):
            Rc = jnp.where(fwd, Rh[gi][rows_f], Rh[gi][rows_b])
            Oc = jnp.where(fwd, Oh[gi][rows_f], Oh[gi][rows_b])
            Gc = jnp.where(fwd, Gs[gi][c_f], Gs[gi][c_b])
            Hc = jnp.where(fwd, Hs[gi][c_f], Hs[gi][c_b])
            Stb = St[gi].astype(BF16)
            outs[gi][step] = _dot(Rc.astype(BF16), Stb) + Oc
            St[gi] = _dot(Gc.astype(BF16), Stb) + Hc
    res = []
    for gi in range(ng):
        o_f = jnp.concatenate(outs[gi], axis=0)
        o_b = jnp.concatenate(outs[gi][::-1], axis=0)
        res.append(jnp.where(fwd, o_f, o_b))
    return res, St


def _rwkv_scan_kernel(r_ref, v_ref, kap_ref, lw_ref, kd_ref, b_ref, o_ref, st_ref):
    d = pl.program_id(2)
    i = pl.program_id(3)
    tm = SEQ_TILE

    @pl.when(i == 0)
    def _():
        st_ref[...] = jnp.zeros_like(st_ref)

    fwd = d == 0
    ri = lax.broadcasted_iota(jnp.int32, (tm, tm), 0)
    ci = lax.broadcasted_iota(jnp.int32, (tm, tm), 1)
    same = (ri // CHUNK) == (ci // CHUNK)
    ahead = jnp.where(fwd, ri - ci, ci - ri)
    strict = same & (ahead > 0)
    incl = same & (ahead >= 0)
    cum_b = jnp.where(incl, 1.0, 0.0).astype(BF16)
    lanes = [slice(grp * LANES, (grp + 1) * LANES) for grp in range(SCAN_GROUPS)]
    groups = [(r_ref[0, :, ls], v_ref[0, :, ls], kap_ref[0, :, ls], lw_ref[0, 0, :, ls],
               kd_ref[0, 0, :, ls], b_ref[0, 0, :, ls]) for ls in lanes]
    outs, sts = _scan_groups(groups, [st_ref[grp] for grp in range(SCAN_GROUPS)], fwd, strict, incl, cum_b)
    for grp, ls in enumerate(lanes):
        o_ref[0, 0, :, ls] = outs[grp]
        st_ref[grp] = sts[grp]


def _rwkv_scan(r, v, kap, lw, kd, b):
    B, S, _ = r.shape
    tm = SEQ_TILE
    n = S // tm
    wb = SCAN_GROUPS * LANES
    pos = lambda d, i: jnp.where(d == 0, i, n - 1 - i)
    tok = pl.BlockSpec((1, tm, wb), lambda bb, hp, d, i: (bb, pos(d, i), hp))
    tok2 = pl.BlockSpec((1, 1, tm, wb), lambda bb, hp, d, i: (d, bb, pos(d, i), hp))
    return pl.pallas_call(
        _rwkv_scan_kernel,
        name="rwkv_scan",
        grid=(B, WIDTH // wb, 2, n),
        in_specs=[tok, tok, tok, tok2, tok2, tok2],
        out_specs=tok2,
        out_shape=jax.ShapeDtypeStruct((2, B, S, WIDTH), F32),
        scratch_shapes=[pltpu.VMEM((SCAN_GROUPS, LANES, LANES), F32)],
        compiler_params=_cparams(("parallel", "parallel", "parallel", "arbitrary")),
    )(r, v, kap, lw, kd, b)


def _retention_kernel(q_ref, k_ref, v_ref, cos_ref, sin_ref, lg_ref, o_ref, st_ref, xi_ref, zt_ref, dm_ref):
    d = pl.program_id(1)
    i = pl.program_id(2)
    tm = SEQ_TILE
    ng = WIDTH // LANES
    groups = [slice(g * LANES, (g + 1) * LANES) for g in range(ng)]
    fwd = d == 0
    lane = lax.broadcasted_iota(jnp.int32, (1, LANES), 1)
    lgl = [jnp.where(lane < HEAD_DIM, lg_ref[2 * g:2 * g + 1, :], lg_ref[2 * g + 1:2 * g + 2, :])
           for g in range(ng)]

    @pl.when(i == 0)
    def _():
        st_ref[...] = jnp.zeros_like(st_ref)
        rowf = lax.broadcasted_iota(jnp.int32, (tm, LANES), 0).astype(F32)
        e_q = jnp.where(fwd, rowf + 1.0, tm - rowf)
        e_k = jnp.where(fwd, tm - 1.0 - rowf, rowf)
        for g in range(ng):
            xi_ref[:, groups[g]] = jnp.exp(e_q * lgl[g])
            zt_ref[:, groups[g]] = jnp.exp(e_k * lgl[g])

    @pl.when(jnp.logical_and(i == 0, fwd))
    def _():
        ri = lax.broadcasted_iota(jnp.int32, (tm, tm), 0)
        ci = lax.broadcasted_iota(jnp.int32, (tm, tm), 1)
        dist = jnp.abs(ri - ci).astype(F32)
        for h in range(N_HEADS):
            dm_ref[h] = jnp.exp(dist * lg_ref[h:h + 1, 0:1])

    cosv = jnp.concatenate([cos_ref[...]] * ng, axis=1)
    sinv = jnp.concatenate([sin_ref[...]] * ng, axis=1)
    half = HEAD_DIM // 2
    lane_w = lax.broadcasted_iota(jnp.int32, (1, WIDTH), 1)
    first = (lane_w % HEAD_DIM) < half

    def rot(t):
        sw = jnp.where(first, pltpu.roll(t, WIDTH - half, 1), pltpu.roll(t, half, 1))
        return t * cosv + sw * sinv

    q = rot(q_ref[0]) * (HEAD_DIM ** -0.5)
    k = rot(k_ref[0])
    v = v_ref[0]
    qx = (q * xi_ref[...]).astype(BF16)
    kz = (k * zt_ref[...]).astype(BF16)
    vb = v.astype(BF16)

    r128 = lax.broadcasted_iota(jnp.int32, (LANES, LANES), 0)
    c128 = lax.broadcasted_iota(jnp.int32, (LANES, LANES), 1)
    bdm = (r128 // HEAD_DIM) == (c128 // HEAD_DIM)
    St = [st_ref[g] for g in range(ng)]
    cross = [_dot(qx[:, groups[g]], St[g].astype(BF16)) for g in range(ng)]
    kv = [_dot_tn(kz[:, groups[g]], vb[:, groups[g]]) for g in range(ng)]
    for g in range(ng):
        st_ref[g] = jnp.exp(tm * lgl[g]) * St[g] + jnp.where(bdm, kv[g], 0.0)

    @pl.when(fwd)
    def _():
        heads = range(N_HEADS)
        mh = [jnp.where((lane // HEAD_DIM) == h % 2, 1.0, 0.0) for h in heads]
        qh = [(q[:, groups[h // 2]] * mh[h]).astype(BF16) for h in heads]
        vh = [(v[:, groups[h // 2]] * mh[h]).astype(BF16) for h in heads]
        kb = [k[:, groups[g]].astype(BF16) for g in range(ng)]
        s = [_dot_nt(qh[h], kb[h // 2]) for h in heads]
        s = [(s[h] * dm_ref[h]).astype(BF16) for h in heads]
        inner = [_dot(s[h], vh[h]) for h in heads]
        for g in range(ng):
            o_ref[0, 0, :, groups[g]] = cross[g] + inner[2 * g] + inner[2 * g + 1]

    @pl.when(jnp.logical_not(fwd))
    def _():
        for g in range(ng):
            o_ref[0, 0, :, groups[g]] = cross[g]


def _retention(p_ret, cos_t, sin_t, lg):
    B, S, _ = p_ret.shape
    tm = SEQ_TILE
    n = S // tm
    pos = lambda d, i: jnp.where(d == 0, i, n - 1 - i)
    col = lambda j: pl.BlockSpec((1, tm, WIDTH), lambda bb, d, i: (bb, pos(d, i), j))
    tab = pl.BlockSpec((tm, LANES), lambda bb, d, i: (pos(d, i), 0))
    return pl.pallas_call(
        _retention_kernel,
        name="retention",
        grid=(B, 2, n),
        in_specs=[col(0), col(1), col(2), tab, tab,
                  pl.BlockSpec((N_HEADS, LANES), lambda bb, d, i: (0, 0))],
        out_specs=pl.BlockSpec((1, 1, tm, WIDTH), lambda bb, d, i: (d, bb, pos(d, i), 0)),
        out_shape=jax.ShapeDtypeStruct((2, B, S, WIDTH), F32),
        scratch_shapes=[pltpu.VMEM((WIDTH // LANES, LANES, LANES), F32),
                        pltpu.VMEM((tm, WIDTH), F32), pltpu.VMEM((tm, WIDTH), F32),
                        pltpu.VMEM((N_HEADS, tm, tm), F32)],
        compiler_params=_cparams(("parallel", "parallel", "arbitrary")),
    )(p_ret, p_ret, p_ret, cos_t, sin_t, lg)


def _head_norm(o, bd, eps):
    mu = _segsum(o, bd) * (1.0 / HEAD_DIM)
    dlt = o - mu
    var = _segsum(dlt * dlt, bd) * (1.0 / HEAD_DIM)
    return dlt * lax.rsqrt(var + eps)


def _merge_kernel(orw_ref, bonus_ref, g_ref, oret_ref, qg_ref, gate_ref, x_ref,
                  lng_ref, lnb_ref, rlng_ref, bd_ref, pa_ref, pb_ref, wo_ref, n2_ref, wq_ref, sk_ref,
                  x1_ref, h2_ref, sc_ref):
    bd = bd_ref[...]
    o = _head_norm(orw_ref[0] + orw_ref[1], bd, RWKV_GN_EPS) * lng_ref[...] + lnb_ref[...]
    o_a = (o + bonus_ref[...]) * g_ref[...]
    o_b = _head_norm(oret_ref[0] + oret_ref[1], bd, RET_GN_EPS) * rlng_ref[...] * jax.nn.silu(qg_ref[...])
    y_a = _dot(o_a.astype(BF16), pa_ref[...])
    y_b = _dot(o_b.astype(BF16), pb_ref[...])
    gate = gate_ref[...]
    mixed = jax.nn.sigmoid(gate[:, :D_MODEL]) * y_a + jax.nn.sigmoid(gate[:, D_MODEL:]) * y_b
    x1 = x_ref[...] + _dot(mixed.astype(BF16), wo_ref[...])
    x1_ref[...] = x1
    h2 = _rmsnorm(x1, n2_ref[...])
    hb = h2.astype(BF16)
    h2_ref[...] = hb
    q = _dot(hb, wq_ref[...]).astype(BF16)
    for j in range(2 * PEER_HEADS):
        sc_ref[j] = _dot_nt(sk_ref[j], q[:, j * 128:(j + 1) * 128])


def _merge(orw, bonus, g, oret, p_ret, p_gate, x2, lng, lnb, rlng, bd, pa, pb, wo, n2, wq, sk):
    T = x2.shape[0]
    tm = TOK_TILE
    full = lambda shape: pl.BlockSpec(shape, lambda i: (0,) * len(shape))
    tok = lambda w: pl.BlockSpec((tm, w), lambda i: (i, 0))
    two = pl.BlockSpec((2, tm, WIDTH), lambda i: (0, i, 0))
    return pl.pallas_call(
        _merge_kernel,
        name="merge",
        grid=(T // tm,),
        in_specs=[two, tok(WIDTH), tok(WIDTH), two,
                  pl.BlockSpec((tm, WIDTH), lambda i: (i, 3)),
                  tok(GATE_COLS), tok(D_MODEL),
                  full((1, WIDTH)), full((1, WIDTH)), full((1, WIDTH)), full((WIDTH, WIDTH)),
                  full((WIDTH, D_MODEL)), full((WIDTH, D_MODEL)), full((D_MODEL, D_MODEL)), full((1, D_MODEL)),
                  full((D_MODEL, 2 * PEER_HEADS * 128)), full((2 * PEER_HEADS, 128, 128))],
        out_specs=[tok(D_MODEL), tok(D_MODEL),
                   pl.BlockSpec((2 * PEER_HEADS, PEER_NKEYS, tm), lambda i: (0, 0, i))],
        out_shape=[jax.ShapeDtypeStruct((T, D_MODEL), F32),
                   jax.ShapeDtypeStruct((T, D_MODEL), BF16),
                   jax.ShapeDtypeStruct((2 * PEER_HEADS, PEER_NKEYS, T), F32)],
        compiler_params=_cparams(("parallel",)),
    )(orw, bonus, g, oret, p_ret, p_gate, x2, lng, lnb, rlng, bd, pa, pb, wo, n2, wq, sk)


def _oddeven_merge(lo, hi, r):
    step = r * 2
    if step < hi - lo:
        yield from _oddeven_merge(lo, hi, step)
        yield from _oddeven_merge(lo + r, hi, step)
        yield from [(i, i + r) for i in range(lo + r, hi - r, step)]
    else:
        yield (lo, lo + r)


def _oddeven_sort(lo, hi):
    if hi - lo >= 1:
        mid = lo + (hi - lo) // 2
        yield from _oddeven_sort(lo, mid)
        yield from _oddeven_sort(mid + 1, hi)
        yield from _oddeven_merge(lo, hi, 1)


_SORT16 = tuple(_oddeven_sort(0, PEER_TOPK - 1))


def _order(x, i, j):
    hi, lo = jnp.maximum(x[i], x[j]), jnp.minimum(x[i], x[j])
    x[i], x[j] = hi, lo


def _top16(s):
    n = PEER_TOPK
    tl = s.shape[1]
    x = [s[8 * a:8 * a + 8, :] for a in range(s.shape[0] // 8)]
    x = x + [jnp.full((8, tl), NEG_BIG, F32)] * (n - len(x))
    for i, j in _SORT16:
        _order(x, i, j)
    for shift in (4, 2, 1):
        r = [pltpu.roll(x[a], shift, 0) for a in range(n)]
        x = [jnp.maximum(x[a], r[n - 1 - a]) for a in range(n)]
        for stride in (8, 4, 2, 1):
            for a in range(n):
                if a & stride == 0:
                    _order(x, a, a + stride)
    sub = lax.broadcasted_iota(jnp.int32, (8, tl), 0)
    halves = []
    for base in (0, 8):
        v = x[base + 7]
        for a in range(6, -1, -1):
            v = jnp.where(sub == a, x[base + a], v)
        halves.append(v)
    return jnp.concatenate(halves, axis=0)


def _prefix_count(pred, rows):
    m8 = pred(rows[7])
    m4 = pred(jnp.where(m8, rows[11], rows[3]))
    lo = jnp.where(m4, rows[5], rows[1])
    hi = jnp.where(m4, rows[13], rows[9])
    m2 = pred(jnp.where(m8, hi, lo))
    a0 = jnp.where(m2, rows[2], rows[0])
    a1 = jnp.where(m2, rows[6], rows[4])
    a2 = jnp.where(m2, rows[10], rows[8])
    a3 = jnp.where(m2, rows[14], rows[12])
    m1 = pred(jnp.where(m8, jnp.where(m4, a3, a2), jnp.where(m4, a1, a0)))
    c = (jnp.where(m8, 8.0, 0.0) + jnp.where(m4, 4.0, 0.0)) + (jnp.where(m2, 2.0, 0.0) + jnp.where(m1, 1.0, 0.0))
    return jnp.where(pred(rows[15]), 16.0, c)


def _topk_prep_kernel(sc_ref, c_ref, e1_ref, rk_ref, e2_ref):
    s1 = sc_ref[0]
    s2 = sc_ref[1]
    v1 = _top16(s1)
    v2 = _top16(s2)
    tl = s1.shape[1]
    cands = [v1[0:1, :] + v2]
    for a in range(1, 8):
        cands.append(v1[a:a + 1, :] + v2[0:8, :])
    cands.append(v1[8:16, :] + v2[0:1, :])
    tau = _top16(jnp.concatenate(cands, axis=0))[PEER_TOPK - 1:PEER_TOPK, :]

    rows = [v2[b:b + 1, :] for b in range(PEER_TOPK)]
    cut = _prefix_count(lambda t: s1 + t >= tau, rows)
    rank2 = _prefix_count(lambda t: t > s2, rows)
    cut16 = jnp.zeros_like(v1)
    for b in range(PEER_TOPK):
        cut16 = jnp.where(v1 + rows[b] >= tau, b + 1.0, cut16)
    e1v = jnp.exp(v1 - v1[0:1, :])
    e2v = jnp.exp(v2 - v2[0:1, :])
    zacc = jnp.zeros_like(v1)
    for b in range(PEER_TOPK):
        zacc = zacc + jnp.where(cut16 > b, e1v, 0.0) * e2v[b:b + 1, :]
    z = jnp.sum(zacc, axis=0, keepdims=True)
    c_ref[0] = cut
    e1_ref[0] = jnp.exp(s1 - v1[0:1, :]) * (0.5 / z)
    rk_ref[0] = rank2.astype(BF16)
    e2_ref[0] = jnp.exp(s2 - v2[0:1, :]).astype(BF16)
    del tl


def _topk_prep(sc):
    _, nk, T = sc.shape
    tl = TOPK_TILE
    blk = pl.BlockSpec((1, nk, tl), lambda h, t: (h, 0, t))
    sf = jax.ShapeDtypeStruct((PEER_HEADS, nk, T), F32)
    sb = jax.ShapeDtypeStruct((PEER_HEADS, nk, T), BF16)
    return pl.pallas_call(
        _topk_prep_kernel,
        name="topk_prep",
        grid=(PEER_HEADS, T // tl),
        in_specs=[pl.BlockSpec((2, nk, tl), lambda h, t: (h, 0, t))],
        out_specs=[blk, blk, blk, blk],
        out_shape=[sf, sf, sb, sb],
        compiler_params=_cparams(("parallel", "parallel")),
    )(sc)


def _gelu2(x):
    return x * (1.0 + lax.erf(x * (2.0 ** -0.5)))


def _peer_kernel(h_ref, u_ref, vt_ref, c_ref, e1_ref, rk_ref, e2_ref, x1_ref, nf_ref, y_ref,
                 acc_ref, act_ref, p_ref):
    e = pl.program_id(1)
    ne = pl.num_programs(1)
    nrow = PEER_EXP_TILE // PEER_NKEYS

    @pl.when(e == 0)
    def _():
        acc_ref[...] = jnp.zeros_like(acc_ref)

    act_ref[...] = _dot_nt(u_ref[...], h_ref[...])
    base = pl.multiple_of(e * nrow, nrow)
    cuts = [c_ref[h, pl.ds(base, nrow), :] for h in range(PEER_HEADS)]
    e1s = [e1_ref[h, pl.ds(base, nrow), :] for h in range(PEER_HEADS)]
    for il in range(nrow):
        rows = slice(il * PEER_NKEYS, (il + 1) * PEER_NKEYS)
        w = None
        for h in range(PEER_HEADS):
            cut = cuts[h][il:il + 1, :].astype(BF16)
            e1 = e1s[h][il:il + 1, :].astype(BF16)
            term = jnp.where(rk_ref[h] < cut, e2_ref[h], jnp.zeros((), BF16)) * e1
            w = term if w is None else w + term
        p_ref[rows, :] = w * _gelu2(act_ref[rows, :].astype(BF16))
    acc_ref[...] += _dot(vt_ref[...], p_ref[...])

    @pl.when(e == ne - 1)
    def _():
        y = x1_ref[...] + acc_ref[...].T
        y_ref[...] = _rmsnorm(y, nf_ref[...])


def _peer(h2, u_bf16, vt_bf16, cut, e1, rank2, e2, x1, nf):
    T = h2.shape[0]
    tT = PEER_TOK_TILE
    eT = PEER_EXP_TILE
    sel = pl.BlockSpec((PEER_HEADS, PEER_NKEYS, tT), lambda t, e: (0, 0, t))
    return pl.pallas_call(
        _peer_kernel,
        name="peer",
        grid=(T // tT, PEER_EXPERTS // eT),
        in_specs=[pl.BlockSpec((tT, D_MODEL), lambda t, e: (t, 0)),
                  pl.BlockSpec((eT, D_MODEL), lambda t, e: (e, 0)),
                  pl.BlockSpec((D_MODEL, eT), lambda t, e: (0, e)),
                  sel, sel, sel, sel,
                  pl.BlockSpec((tT, D_MODEL), lambda t, e: (t, 0)),
                  pl.BlockSpec((1, D_MODEL), lambda t, e: (0, 0))],
        out_specs=pl.BlockSpec((tT, D_MODEL), lambda t, e: (t, 0)),
        out_shape=jax.ShapeDtypeStruct((T, D_MODEL), F32),
        scratch_shapes=[pltpu.VMEM((D_MODEL, tT), F32),
                        pltpu.VMEM((eT, tT), F32),
                        pltpu.VMEM((eT, tT), BF16)],
        compiler_params=_cparams(("parallel", "arbitrary")),
    )(h2, u_bf16, vt_bf16, cut, e1, rank2, e2, x1, nf)


def _rope_tables(S):
    half = HEAD_DIM // 2
    inv = ROPE_BASE ** (-jnp.arange(half, dtype=F32) / half)
    ang = jnp.arange(S, dtype=F32)[:, None] * inv[None, :]
    c = jnp.cos(ang)
    s = jnp.sin(ang)
    reps = LANES // HEAD_DIM
    cos_t = jnp.tile(jnp.concatenate([c, c], axis=1), (1, reps))
    sin_t = jnp.tile(jnp.concatenate([-s, s], axis=1), (1, reps))
    return cos_t, sin_t


def _trunk(x, w):
    B, S, _ = x.shape
    T = B * S
    x2 = x.reshape(T, D_MODEL)
    p_rwkv, p_ret, p_gate = _inproj(x2, w["norm1_g"], w["w_in"])
    r, v, kap, lw, kd, b, bonus, g = _rwkv_prep(
        p_rwkv.reshape(B, S, RWKV_COLS), w["mu"], w["w0"], w["w2"], w["a0"], w["a2"], w["g2"],
        w["k_k"], w["k_a"], w["r_k"], w["bd"])
    orw = _rwkv_scan(r, v, kap, lw, kd, b)
    cos_t, sin_t = _rope_tables(S)
    oret = _retention(p_ret.reshape(B, S, RET_COLS), cos_t, sin_t, w["lg"])
    x1, h2, sc = _merge(orw.reshape(2, T, WIDTH), bonus.reshape(T, WIDTH), g.reshape(T, WIDTH),
                        oret.reshape(2, T, WIDTH), p_ret, p_gate, x2,
                        w["ln_g"], w["ln_b"], w["ret_ln_g"], w["bd"], w["proj_a"], w["proj_b"], w["w_out"],
                        w["norm2_g"], w["wq"], w["sk"])
    cut, e1, rank2, e2 = _topk_prep(sc)
    y = _peer(h2, w["u"], w["vt"], cut, e1, rank2, e2, x1, w["normf_g"])
    return y.reshape(B, S, D_MODEL)


def kernel(x_prompt, x_sample, norm1_g, w_in, rwkv_mu, rwkv_w0, rwkv_w2, rwkv_a0, rwkv_a2, rwkv_g2, rwkv_k_k,
           rwkv_k_a, rwkv_r_k, rwkv_ln_g, rwkv_ln_b, ret_ln_g, proj_a, proj_b, w_out, norm2_g, peer_wq,
           peer_subkeys, peer_u, peer_v, normf_g):
    assert norm1_g.shape[0] == 1, "single-layer trunk"
    head_id = jnp.arange(WIDTH) // HEAD_DIM
    w = {
        "norm1_g": norm1_g[0][None, :],
        "w_in": w_in[0].astype(BF16),
        "mu": rwkv_mu[0],
        "w0": rwkv_w0[0],
        "w2": rwkv_w2[0].astype(BF16),
        "a0": rwkv_a0[0],
        "a2": rwkv_a2[0].astype(BF16),
        "g2": rwkv_g2[0].astype(BF16),
        "k_k": rwkv_k_k[0][None, :],
        "k_a": rwkv_k_a[0][None, :],
        "r_k": rwkv_r_k[0].reshape(1, WIDTH),
        "ln_g": rwkv_ln_g[0][None, :],
        "ln_b": rwkv_ln_b[0][None, :],
        "ret_ln_g": ret_ln_g[0][None, :],
        "bd": (head_id[:, None] == head_id[None, :]).astype(BF16),
        "lg": jnp.broadcast_to(jnp.log1p(-jnp.exp2(-5.0 - jnp.arange(N_HEADS, dtype=F32)))[:, None],
                               (N_HEADS, LANES)),
        "proj_a": proj_a[0].astype(BF16),
        "proj_b": proj_b[0].astype(BF16),
        "w_out": w_out[0].astype(BF16),
        "norm2_g": norm2_g[0][None, :],
        "wq": peer_wq[0].astype(BF16),
        "sk": peer_subkeys[0].reshape(2 * PEER_HEADS, PEER_NKEYS, 128).astype(BF16),
        "u": peer_u[0].astype(BF16),
        "vt": peer_v[0].T.astype(BF16),
        "normf_g": normf_g[None, :],
    }
    return (_trunk(x_prompt, w), _trunk(x_sample, w))
```

```python
import functools

import jax
import jax.numpy as jnp
from jax import lax
from jax.experimental import pallas as pl
from jax.experimental.pallas import tpu as pltpu

F32 = jnp.float32
BF16 = jnp.bfloat16

D_MODEL = 1024
HEAD_DIM = 64
N_HEADS = 8
WIDTH = N_HEADS * HEAD_DIM
RWKV_COLS = 1920
RET_COLS = 2048
GATE_COLS = 2048
RWKV_GN_EPS = 64e-5
RET_GN_EPS = 1e-6
NORM_EPS = 1e-6
ROPE_BASE = 10000.0
PEER_HEADS = 8
PEER_NKEYS = 128
PEER_TOPK = 16
PEER_EXPERTS = PEER_NKEYS * PEER_NKEYS

LANES = 128
CHUNK = 64
SEQ_TILE = 256
SCAN_GROUPS = 4
TOK_TILE = 256
PEER_TOK_TILE = 512
PEER_EXP_TILE = 2048
TOPK_TILE = 512
VMEM_LIMIT = 56 * 1024 * 1024
NEG_BIG = -1e30

_NT = (((1,), (1,)), ((), ()))


def _cparams(sem):
    return pltpu.CompilerParams(dimension_semantics=sem, vmem_limit_bytes=VMEM_LIMIT)


def _dot(a, b):
    return jnp.dot(a, b, preferred_element_type=F32)


def _dot_nt(a, b):
    return lax.dot_general(a, b, _NT, preferred_element_type=F32)


def _dot_tn(a, b):
    return lax.dot_general(a, b, (((0,), (0,)), ((), ())), preferred_element_type=F32)


def _split2(x):
    hi = x.astype(BF16)
    lo = (x - hi.astype(F32)).astype(BF16)
    return hi, lo


def _ones_mm(m_bf16, x):
    x0 = x.astype(BF16)
    r1 = x - x0.astype(F32)
    x1 = r1.astype(BF16)
    x2 = (r1 - x1.astype(F32)).astype(BF16)
    return _dot(m_bf16, x0) + (_dot(m_bf16, x1) + _dot(m_bf16, x2))


def _segsum(x, bd):
    hi, lo = _split2(x)
    return _dot(hi, bd) + _dot(lo, bd)


def _rmsnorm(x, g):
    return x * lax.rsqrt(jnp.mean(x * x, axis=-1, keepdims=True) + NORM_EPS) * g


def _rotary(t, cos_ref, sin_ref):
    reps = WIDTH // LANES
    cosv = jnp.concatenate([cos_ref[...]] * reps, axis=1)
    sinv = jnp.concatenate([sin_ref[...]] * reps, axis=1)
    half = HEAD_DIM // 2
    first = (lax.broadcasted_iota(jnp.int32, (1, WIDTH), 1) % HEAD_DIM) < half
    sw = jnp.where(first, pltpu.roll(t, WIDTH - half, 1), pltpu.roll(t, half, 1))
    return t * cosv + sw * sinv


def _inproj_kernel(x_ref, g_ref, w_ref, cos_ref, sin_ref, o1_ref, o2_ref, o3_ref):
    hb = _rmsnorm(x_ref[...], g_ref[...]).astype(BF16)
    o1_ref[...] = _dot(hb, w_ref[:, 0:RWKV_COLS])
    o2_ref[:, 0:WIDTH] = _rotary(_dot(hb, w_ref[:, RWKV_COLS:RWKV_COLS + WIDTH]), cos_ref, sin_ref)
    o2_ref[:, WIDTH:2 * WIDTH] = _rotary(_dot(hb, w_ref[:, RWKV_COLS + WIDTH:RWKV_COLS + 2 * WIDTH]), cos_ref, sin_ref)
    o2_ref[:, 2 * WIDTH:] = _dot(hb, w_ref[:, RWKV_COLS + 2 * WIDTH:RWKV_COLS + RET_COLS])
    o3_ref[...] = _dot(hb, w_ref[:, RWKV_COLS + RET_COLS:])


def _inproj(x2, g, w_bf16, cos_t, sin_t):
    T = x2.shape[0]
    tm = TOK_TILE
    ncol = w_bf16.shape[1]
    nseq = cos_t.shape[0] // tm
    tab = pl.BlockSpec((tm, LANES), lambda i: (i % nseq, 0))
    return pl.pallas_call(
        _inproj_kernel,
        name="inproj",
        grid=(T // tm,),
        in_specs=[
            pl.BlockSpec((tm, D_MODEL), lambda i: (i, 0)),
            pl.BlockSpec((1, D_MODEL), lambda i: (0, 0)),
            pl.BlockSpec((D_MODEL, ncol), lambda i: (0, 0)),
            tab, tab,
        ],
        out_specs=[
            pl.BlockSpec((tm, RWKV_COLS), lambda i: (i, 0)),
            pl.BlockSpec((tm, RET_COLS), lambda i: (i, 0)),
            pl.BlockSpec((tm, GATE_COLS), lambda i: (i, 0)),
        ],
        out_shape=[
            jax.ShapeDtypeStruct((T, RWKV_COLS), F32),
            jax.ShapeDtypeStruct((T, RET_COLS), F32),
            jax.ShapeDtypeStruct((T, GATE_COLS), F32),
        ],
        compiler_params=_cparams(("parallel",)),
    )(x2, g, w_bf16, cos_t, sin_t)


def _softplus(y):
    return jnp.maximum(y, 0.0) + jnp.log1p(jnp.exp(-jnp.abs(y)))


def _rwkv_prep_kernel(p_ref, pprev_ref, pnext_ref, mu_ref, w0_ref, w2_ref, a0_ref, a2_ref, g2_ref,
                      kk_ref, ka_ref, rk_ref, bd_ref,
                      r_ref, v_ref, kap_ref, lw_ref, kd_ref, b_ref, bonus_ref, g_ref):
    i = pl.program_id(1)
    n = pl.num_programs(1)
    p = p_ref[0]
    tm = p.shape[0]
    prev_row = jnp.where(i > 0, pprev_ref[0, 7:8, :], 0.0)
    next_row = jnp.where(i < n - 1, pnext_ref[0, 0:1, :], 0.0)
    row = lax.broadcasted_iota(jnp.int32, p.shape, 0)
    prev = jnp.where(row == 0, prev_row, pltpu.roll(p, 1, 0))
    nxt = jnp.where(row == tm - 1, next_row, pltpu.roll(p, tm - 1, 0))
    ps = p + mu_ref[0:1, :] * (prev - p) + mu_ref[1:2, :] * (nxt - p)

    bd = bd_ref[...]
    r = ps[:, 0:WIDTH]
    k = ps[:, WIDTH:2 * WIDTH]
    v = ps[:, 2 * WIDTH:3 * WIDTH]
    kk = k * kk_ref[...]
    kappa = kk / jnp.maximum(jnp.sqrt(_segsum(kk * kk, bd)), 1e-12)
    r_ref[0] = r
    v_ref[0] = v
    kap_ref[0] = kappa
    ksum = jnp.zeros_like(k)
    for d in range(2):
        wl = ps[:, 1536 + 64 * d:1600 + 64 * d]
        al = ps[:, 1664 + 64 * d:1728 + 64 * d]
        z = w0_ref[d:d + 1, :] + _dot(jnp.tanh(wl).astype(BF16), w2_ref[d])
        w_log = -_softplus(-z) - 0.5
        lw_ref[d, 0] = -jnp.exp(w_log)
        a = jax.nn.sigmoid(a0_ref[d:d + 1, :] + _dot(al.astype(BF16), a2_ref[d]))
        kd = k * (1.0 + (a - 1.0) * ka_ref[...])
        kd_ref[d, 0] = kd
        b_ref[d, 0] = kappa * a
        ksum = ksum + kd
    bonus_ref[0] = _segsum(r * (0.5 * ksum) * rk_ref[...], bd) * v
    g_ref[0] = _dot(jax.nn.sigmoid(ps[:, 1792:1920]).astype(BF16), g2_ref[...])


def _rwkv_prep(p_rwkv, mu, w0, w2, a0, a2, g2, k_k, k_a, r_k, bd):
    B, S, _ = p_rwkv.shape
    tm = SEQ_TILE
    n = S // tm
    nb8 = S // 8
    full = lambda shape: pl.BlockSpec(shape, lambda b, i: (0,) * len(shape))
    tok = pl.BlockSpec((1, tm, WIDTH), lambda b, i: (b, i, 0))
    tok2 = pl.BlockSpec((2, 1, tm, WIDTH), lambda b, i: (0, b, i, 0))
    s1 = jax.ShapeDtypeStruct((B, S, WIDTH), F32)
    s2 = jax.ShapeDtypeStruct((2, B, S, WIDTH), F32)
    return pl.pallas_call(
        _rwkv_prep_kernel,
        name="rwkv_prep",
        grid=(B, n),
        in_specs=[
            pl.BlockSpec((1, tm, RWKV_COLS), lambda b, i: (b, i, 0)),
            pl.BlockSpec((1, 8, RWKV_COLS), lambda b, i: (b, jnp.maximum(i * (tm // 8) - 1, 0), 0)),
            pl.BlockSpec((1, 8, RWKV_COLS), lambda b, i: (b, jnp.minimum((i + 1) * (tm // 8), nb8 - 1), 0)),
            full((2, RWKV_COLS)), full((2, WIDTH)), full((2, 64, WIDTH)), full((2, WIDTH)),
            full((2, 64, WIDTH)), full((128, WIDTH)), full((1, WIDTH)), full((1, WIDTH)), full((1, WIDTH)),
            full((WIDTH, WIDTH)),
        ],
        out_specs=[tok, tok, tok, tok2, tok2, tok2, tok, tok],
        out_shape=[s1, s1, s1, s2, s2, s2, s1, s1],
        compiler_params=_cparams(("parallel", "parallel")),
    )(p_rwkv, p_rwkv, p_rwkv, mu, w0, w2, a0, a2, g2, k_k, k_a, r_k, bd)


def _scan_groups(groups, states, fwd, strict, incl, cum_b):
    tm = SEQ_TILE
    nck = tm // CHUNK
    ng = len(groups)
    lane = lax.broadcasted_iota(jnp.int32, (1, LANES), 1)
    gs, RRb, Ath, Rth, Vhb = [], [], [], [], []
    for (R, V, KAP, LW, KD, BV) in groups:
        g = _ones_mm(cum_b, LW)
        eng = jnp.exp(-g)
        At = -KAP * jnp.exp(g - LW)
        Rt = R * jnp.exp(g)
        gs.append(g)
        RRb.append(jnp.concatenate([BV * eng, KD * eng], axis=0).astype(BF16))
        for h in range(2):
            mh = jnp.where((lane // HEAD_DIM) == h, 1.0, 0.0)
            Ath.append(At * mh)
            Rth.append(Rt * mh)
            Vhb.append((V * mh).astype(BF16))
    chains = range(2 * ng)
    Q = [_dot_nt(jnp.concatenate([Ath[c], Rth[c]], axis=0).astype(BF16), RRb[c // 2]) for c in chains]
    Nb = [jnp.where(strict, Q[c][:tm, :tm], 0.0).astype(BF16) for c in chains]
    Aak = [jnp.where(strict, Q[c][:tm, tm:], 0.0).astype(BF16) for c in chains]
    Arb = [jnp.where(incl, Q[c][tm:, :tm], 0.0).astype(BF16) for c in chains]
    Ark = [jnp.where(incl, Q[c][tm:, tm:], 0.0).astype(BF16) for c in chains]
    X = [jnp.concatenate([Ath[c], _dot(Aak[c], Vhb[c])], axis=1) for c in chains]
    X = [X[c] + _dot(Nb[c], X[c].astype(BF16)) for c in chains]
    Pb = Nb
    for _ in range(5):
        Pb = [_dot(Pb[c], Pb[c]).astype(BF16) for c in chains]
        X = [X[c] + _dot(Pb[c], X[c].astype(BF16)) for c in chains]
    Rhh = [Rth[c] + _dot(Arb[c], X[c][:, :LANES].astype(BF16)) for c in chains]
    Ohh = [_dot(Arb[c], X[c][:, LANES:].astype(BF16)) + _dot(Ark[c], Vhb[c]) for c in chains]

    r128 = lax.broadcasted_iota(jnp.int32, (LANES, LANES), 0)
    c128 = lax.broadcasted_iota(jnp.int32, (LANES, LANES), 1)
    bdm = (r128 // HEAD_DIM) == (c128 // HEAD_DIM)
    eye = r128 == c128
    colc = lax.broadcasted_iota(jnp.int32, (1, tm), 1) // CHUNK
    Rh, Oh, Gs, Hs = [], [], [], []
    for gi, (R, V, KAP, LW, KD, BV) in enumerate(groups):
        g = gs[gi]
        Wtb = (X[2 * gi][:, :LANES] + X[2 * gi + 1][:, :LANES]).astype(BF16)
        Utb = (X[2 * gi][:, LANES:] + X[2 * gi + 1][:, LANES:]).astype(BF16)
        Rh.append(Rhh[2 * gi] + Rhh[2 * gi + 1])
        Oh.append(Ohh[2 * gi] + Ohh[2 * gi + 1])
        ends = []
        for c in range(nck):
            e_f = g[c * CHUNK + CHUNK - 1:c * CHUNK + CHUNK, :]
            e_b = g[c * CHUNK:c * CHUNK + 1, :]
            ends.append(jnp.where(fwd, e_f, e_b))
        gend = jnp.concatenate([jnp.broadcast_to(e, (CHUNK, LANES)) for e in ends], axis=0)
        egc = jnp.exp(gend - g)
        BhT = (BV * egc).T.astype(BF16)
        KhT = (KD * egc).T.astype(BF16)
        Vb = V.astype(BF16)
        Gg, Hg = [], []
        for c in range(nck):
            cm = colc == c
            Bc = jnp.where(cm, BhT, jnp.zeros((), BF16))
            Kc = jnp.where(cm, KhT, jnp.zeros((), BF16))
            Gg.append(jnp.where(eye, jnp.exp(ends[c]), 0.0) + jnp.where(bdm, _dot(Bc, Wtb), 0.0))
            Hg.append(jnp.where(bdm, _dot(Bc, Utb) + _dot(Kc, Vb), 0.0))
        Gs.append(Gg)
        Hs.append(Hg)

    St = list(states)
    outs = [[None] * nck for _ in range(ng)]
    for step in range(nck):
        c_f = step
        c_b = nck - 1 - step
        rows_f = slice(c_f * CHUNK, (c_f + 1) * CHUNK)
        rows_b = slice(c_b * CHUNK, (c_b + 1) * CHUNK)
        for gi in range(ng):
            Rc = jnp.where(fwd, Rh[gi][rows_f], Rh[gi][rows_b])
            Oc = jnp.where(fwd, Oh[gi][rows_f], Oh[gi][rows_b])
            Gc = jnp.where(fwd, Gs[gi][c_f], Gs[gi][c_b])
            Hc = jnp.where(fwd, Hs[gi][c_f], Hs[gi][c_b])
            Stb = St[gi].astype(BF16)
            outs[gi][step] = _dot(Rc.astype(BF16), Stb) + Oc
            St[gi] = _dot(Gc.astype(BF16), Stb) + Hc
    res = []
    for gi in range(ng):
        o_f = jnp.concatenate(outs[gi], axis=0)
        o_b = jnp.concatenate(outs[gi][::-1], axis=0)
        res.append(jnp.where(fwd, o_f, o_b))
    return res, St


def _rwkv_scan_kernel(r_ref, v_ref, kap_ref, lw_ref, kd_ref, b_ref, o_ref, st_ref):
    d = pl.program_id(2)
    i = pl.program_id(3)
    tm = SEQ_TILE

    @pl.when(i == 0)
    def _():
        st_ref[...] = jnp.zeros_like(st_ref)

    fwd = d == 0
    ri = lax.broadcasted_iota(jnp.int32, (tm, tm), 0)
    ci = lax.broadcasted_iota(jnp.int32, (tm, tm), 1)
    same = (ri // CHUNK) == (ci // CHUNK)
    ahead = jnp.where(fwd, ri - ci, ci - ri)
    strict = same & (ahead > 0)
    incl = same & (ahead >= 0)
    cum_b = jnp.where(incl, 1.0, 0.0).astype(BF16)
    lanes = [slice(grp * LANES, (grp + 1) * LANES) for grp in range(SCAN_GROUPS)]
    groups = [(r_ref[0, :, ls], v_ref[0, :, ls], kap_ref[0, :, ls], lw_ref[0, 0, :, ls],
               kd_ref[0, 0, :, ls], b_ref[0, 0, :, ls]) for ls in lanes]
    outs, sts = _scan_groups(groups, [st_ref[grp] for grp in range(SCAN_GROUPS)], fwd, strict, incl, cum_b)
    for grp, ls in enumerate(lanes):
        o_ref[0, 0, :, ls] = outs[grp]
        st_ref[grp] = sts[grp]


def _rwkv_scan(r, v, kap, lw, kd, b):
    B, S, _ = r.shape
    tm = SEQ_TILE
    n = S // tm
    wb = SCAN_GROUPS * LANES
    pos = lambda d, i: jnp.where(d == 0, i, n - 1 - i)
    tok = pl.BlockSpec((1, tm, wb), lambda bb, hp, d, i: (bb, pos(d, i), hp))
    tok2 = pl.BlockSpec((1, 1, tm, wb), lambda bb, hp, d, i: (d, bb, pos(d, i), hp))
    return pl.pallas_call(
        _rwkv_scan_kernel,
        name="rwkv_scan",
        grid=(B, WIDTH // wb, 2, n),
        in_specs=[tok, tok, tok, tok2, tok2, tok2],
        out_specs=tok2,
        out_shape=jax.ShapeDtypeStruct((2, B, S, WIDTH), F32),
        scratch_shapes=[pltpu.VMEM((SCAN_GROUPS, LANES, LANES), F32)],
        compiler_params=_cparams(("parallel", "parallel", "parallel", "arbitrary")),
    )(r, v, kap, lw, kd, b)


def _retention_kernel(q_ref, k_ref, v_ref, lg_ref, o_ref, st_ref, xi_ref, zt_ref, dm_ref):
    d = pl.program_id(1)
    i = pl.program_id(2)
    tm = SEQ_TILE
    ng = WIDTH // LANES
    groups = [slice(g * LANES, (g + 1) * LANES) for g in range(ng)]
    fwd = d == 0
    lane = lax.broadcasted_iota(jnp.int32, (1, LANES), 1)
    lgl = [jnp.where(lane < HEAD_DIM, lg_ref[2 * g:2 * g + 1, :], lg_ref[2 * g + 1:2 * g + 2, :])
           for g in range(ng)]

    @pl.when(i == 0)
    def _():
        st_ref[...] = jnp.zeros_like(st_ref)
        rowf = lax.broadcasted_iota(jnp.int32, (tm, LANES), 0).astype(F32)
        e_q = jnp.where(fwd, rowf + 1.0, tm - rowf)
        e_k = jnp.where(fwd, tm - 1.0 - rowf, rowf)
        for g in range(ng):
            xi_ref[:, groups[g]] = jnp.exp(e_q * lgl[g])
            zt_ref[:, groups[g]] = jnp.exp(e_k * lgl[g])

    @pl.when(jnp.logical_and(i == 0, fwd))
    def _():
        ri = lax.broadcasted_iota(jnp.int32, (tm, tm), 0)
        ci = lax.broadcasted_iota(jnp.int32, (tm, tm), 1)
        dist = jnp.abs(ri - ci).astype(F32)
        for h in range(N_HEADS):
            dm_ref[h] = jnp.exp(dist * lg_ref[h:h + 1, 0:1])

    q = q_ref[0] * (HEAD_DIM ** -0.5)
    k = k_ref[0]
    v = v_ref[0]
    qx = (q * xi_ref[...]).astype(BF16)
    kz = (k * zt_ref[...]).astype(BF16)
    vb = v.astype(BF16)

    r128 = lax.broadcasted_iota(jnp.int32, (LANES, LANES), 0)
    c128 = lax.broadcasted_iota(jnp.int32, (LANES, LANES), 1)
    bdm = (r128 // HEAD_DIM) == (c128 // HEAD_DIM)
    St = [st_ref[g] for g in range(ng)]
    cross = [_dot(qx[:, groups[g]], St[g].astype(BF16)) for g in range(ng)]
    kv = [_dot_tn(kz[:, groups[g]], vb[:, groups[g]]) for g in range(ng)]
    for g in range(ng):
        st_ref[g] = jnp.exp(tm * lgl[g]) * St[g] + jnp.where(bdm, kv[g], 0.0)

    @pl.when(fwd)
    def _():
        heads = range(N_HEADS)
        mh = [jnp.where((lane // HEAD_DIM) == h % 2, 1.0, 0.0) for h in heads]
        qh = [(q[:, groups[h // 2]] * mh[h]).astype(BF16) for h in heads]
        vh = [(v[:, groups[h // 2]] * mh[h]).astype(BF16) for h in heads]
        kb = [k[:, groups[g]].astype(BF16) for g in range(ng)]
        s = [_dot_nt(qh[h], kb[h // 2]) for h in heads]
        s = [(s[h] * dm_ref[h]).astype(BF16) for h in heads]
        inner = [_dot(s[h], vh[h]) for h in heads]
        for g in range(ng):
            o_ref[0, 0, :, groups[g]] = cross[g] + inner[2 * g] + inner[2 * g + 1]

    @pl.when(jnp.logical_not(fwd))
    def _():
        for g in range(ng):
            o_ref[0, 0, :, groups[g]] = cross[g]


def _retention(p_ret, lg):
    B, S, _ = p_ret.shape
    tm = SEQ_TILE
    n = S // tm
    pos = lambda d, i: jnp.where(d == 0, i, n - 1 - i)
    col = lambda j: pl.BlockSpec((1, tm, WIDTH), lambda bb, d, i: (bb, pos(d, i), j))
    return pl.pallas_call(
        _retention_kernel,
        name="retention",
        grid=(B, 2, n),
        in_specs=[col(0), col(1), col(2),
                  pl.BlockSpec((N_HEADS, LANES), lambda bb, d, i: (0, 0))],
        out_specs=pl.BlockSpec((1, 1, tm, WIDTH), lambda bb, d, i: (d, bb, pos(d, i), 0)),
        out_shape=jax.ShapeDtypeStruct((2, B, S, WIDTH), F32),
        scratch_shapes=[pltpu.VMEM((WIDTH // LANES, LANES, LANES), F32),
                        pltpu.VMEM((tm, WIDTH), F32), pltpu.VMEM((tm, WIDTH), F32),
                        pltpu.VMEM((N_HEADS, tm, tm), F32)],
        compiler_params=_cparams(("parallel", "parallel", "arbitrary")),
    )(p_ret, p_ret, p_ret, lg)


def _head_norm(o, bd, eps):
    mu = _segsum(o, bd) * (1.0 / HEAD_DIM)
    dlt = o - mu
    var = _segsum(dlt * dlt, bd) * (1.0 / HEAD_DIM)
    return dlt * lax.rsqrt(var + eps)


def _merge_kernel(orw_ref, bonus_ref, g_ref, oret_ref, qg_ref, gate_ref, x_ref,
                  lng_ref, lnb_ref, rlng_ref, bd_ref, pa_ref, pb_ref, wo_ref, n2_ref, wq_ref, sk_ref,
                  x1_ref, h2_ref, sc_ref):
    bd = bd_ref[...]
    o = _head_norm(orw_ref[0] + orw_ref[1], bd, RWKV_GN_EPS) * lng_ref[...] + lnb_ref[...]
    o_a = (o + bonus_ref[...]) * g_ref[...]
    o_b = _head_norm(oret_ref[0] + oret_ref[1], bd, RET_GN_EPS) * rlng_ref[...] * jax.nn.silu(qg_ref[...])
    y_a = _dot(o_a.astype(BF16), pa_ref[...])
    y_b = _dot(o_b.astype(BF16), pb_ref[...])
    gate = gate_ref[...]
    mixed = jax.nn.sigmoid(gate[:, :D_MODEL]) * y_a + jax.nn.sigmoid(gate[:, D_MODEL:]) * y_b
    x1 = x_ref[...] + _dot(mixed.astype(BF16), wo_ref[...])
    x1_ref[...] = x1
    h2 = _rmsnorm(x1, n2_ref[...])
    hb = h2.astype(BF16)
    h2_ref[...] = hb
    q = _dot(hb, wq_ref[...]).astype(BF16)
    for j in range(2 * PEER_HEADS):
        sc_ref[j] = _dot_nt(sk_ref[j], q[:, j * 128:(j + 1) * 128])


def _merge(orw, bonus, g, oret, p_ret, p_gate, x2, lng, lnb, rlng, bd, pa, pb, wo, n2, wq, sk):
    T = x2.shape[0]
    tm = TOK_TILE
    full = lambda shape: pl.BlockSpec(shape, lambda i: (0,) * len(shape))
    tok = lambda w: pl.BlockSpec((tm, w), lambda i: (i, 0))
    two = pl.BlockSpec((2, tm, WIDTH), lambda i: (0, i, 0))
    return pl.pallas_call(
        _merge_kernel,
        name="merge",
        grid=(T // tm,),
        in_specs=[two, tok(WIDTH), tok(WIDTH), two,
                  pl.BlockSpec((tm, WIDTH), lambda i: (i, 3)),
                  tok(GATE_COLS), tok(D_MODEL),
                  full((1, WIDTH)), full((1, WIDTH)), full((1, WIDTH)), full((WIDTH, WIDTH)),
                  full((WIDTH, D_MODEL)), full((WIDTH, D_MODEL)), full((D_MODEL, D_MODEL)), full((1, D_MODEL)),
                  full((D_MODEL, 2 * PEER_HEADS * 128)), full((2 * PEER_HEADS, 128, 128))],
        out_specs=[tok(D_MODEL), tok(D_MODEL),
                   pl.BlockSpec((2 * PEER_HEADS, PEER_NKEYS, tm), lambda i: (0, 0, i))],
        out_shape=[jax.ShapeDtypeStruct((T, D_MODEL), F32),
                   jax.ShapeDtypeStruct((T, D_MODEL), BF16),
                   jax.ShapeDtypeStruct((2 * PEER_HEADS, PEER_NKEYS, T), F32)],
        compiler_params=_cparams(("parallel",)),
    )(orw, bonus, g, oret, p_ret, p_gate, x2, lng, lnb, rlng, bd, pa, pb, wo, n2, wq, sk)


def _oddeven_merge(lo, hi, r):
    step = r * 2
    if step < hi - lo:
        yield from _oddeven_merge(lo, hi, step)
        yield from _oddeven_merge(lo + r, hi, step)
        yield from [(i, i + r) for i in range(lo + r, hi - r, step)]
    else:
        yield (lo, lo + r)


def _oddeven_sort(lo, hi):
    if hi - lo >= 1:
        mid = lo + (hi - lo) // 2
        yield from _oddeven_sort(lo, mid)
        yield from _oddeven_sort(mid + 1, hi)
        yield from _oddeven_merge(lo, hi, 1)


_SORT16 = tuple(_oddeven_sort(0, PEER_TOPK - 1))


def _order(x, i, j):
    hi, lo = jnp.maximum(x[i], x[j]), jnp.minimum(x[i], x[j])
    x[i], x[j] = hi, lo


def _top16(s):
    n = PEER_TOPK
    tl = s.shape[1]
    x = [s[8 * a:8 * a + 8, :] for a in range(s.shape[0] // 8)]
    x = x + [jnp.full((8, tl), NEG_BIG, F32)] * (n - len(x))
    for i, j in _SORT16:
        _order(x, i, j)
    for shift in (4, 2, 1):
        r = [pltpu.roll(x[a], shift, 0) for a in range(n)]
        x = [jnp.maximum(x[a], r[n - 1 - a]) for a in range(n)]
        for stride in (8, 4, 2, 1):
            for a in range(n):
                if a & stride == 0:
                    _order(x, a, a + stride)
    sub = lax.broadcasted_iota(jnp.int32, (8, tl), 0)
    halves = []
    for base in (0, 8):
        v = x[base + 7]
        for a in range(6, -1, -1):
            v = jnp.where(sub == a, x[base + a], v)
        halves.append(v)
    return jnp.concatenate(halves, axis=0)


def _prefix_count(pred, rows):
    m8 = pred(rows[7])
    m4 = pred(jnp.where(m8, rows[11], rows[3]))
    lo = jnp.where(m4, rows[5], rows[1])
    hi = jnp.where(m4, rows[13], rows[9])
    m2 = pred(jnp.where(m8, hi, lo))
    a0 = jnp.where(m2, rows[2], rows[0])
    a1 = jnp.where(m2, rows[6], rows[4])
    a2 = jnp.where(m2, rows[10], rows[8])
    a3 = jnp.where(m2, rows[14], rows[12])
    m1 = pred(jnp.where(m8, jnp.where(m4, a3, a2), jnp.where(m4, a1, a0)))
    c = (jnp.where(m8, 8.0, 0.0) + jnp.where(m4, 4.0, 0.0)) + (jnp.where(m2, 2.0, 0.0) + jnp.where(m1, 1.0, 0.0))
    return jnp.where(pred(rows[15]), 16.0, c)


def _topk_prep_kernel(sc_ref, c_ref, e1_ref, rk_ref, e2_ref):
    s1 = sc_ref[0]
    s2 = sc_ref[1]
    v1 = _top16(s1)
    v2 = _top16(s2)
    tl = s1.shape[1]
    cands = [v1[0:1, :] + v2]
    for a in range(1, 8):
        cands.append(v1[a:a + 1, :] + v2[0:8, :])
    cands.append(v1[8:16, :] + v2[0:1, :])
    tau = _top16(jnp.concatenate(cands, axis=0))[PEER_TOPK - 1:PEER_TOPK, :]

    rows = [v2[b:b + 1, :] for b in range(PEER_TOPK)]
    cut = _prefix_count(lambda t: s1 + t >= tau, rows)
    rank2 = _prefix_count(lambda t: t > s2, rows)
    cut16 = jnp.zeros_like(v1)
    for b in range(PEER_TOPK):
        cut16 = jnp.where(v1 + rows[b] >= tau, b + 1.0, cut16)
    e1v = jnp.exp(v1 - v1[0:1, :])
    e2v = jnp.exp(v2 - v2[0:1, :])
    zacc = jnp.zeros_like(v1)
    for b in range(PEER_TOPK):
        zacc = zacc + jnp.where(cut16 > b, e1v, 0.0) * e2v[b:b + 1, :]
    z = jnp.sum(zacc, axis=0, keepdims=True)
    c_ref[0] = cut
    e1_ref[0] = jnp.exp(s1 - v1[0:1, :]) * (0.5 / z)
    rk_ref[0] = rank2.astype(BF16)
    e2_ref[0] = jnp.exp(s2 - v2[0:1, :]).astype(BF16)
    del tl


def _topk_prep(sc):
    _, nk, T = sc.shape
    tl = TOPK_TILE
    blk = pl.BlockSpec((1, nk, tl), lambda h, t: (h, 0, t))
    sf = jax.ShapeDtypeStruct((PEER_HEADS, nk, T), F32)
    sb = jax.ShapeDtypeStruct((PEER_HEADS, nk, T), BF16)
    return pl.pallas_call(
        _topk_prep_kernel,
        name="topk_prep",
        grid=(PEER_HEADS, T // tl),
        in_specs=[pl.BlockSpec((2, nk, tl), lambda h, t: (h, 0, t))],
        out_specs=[blk, blk, blk, blk],
        out_shape=[sf, sf, sb, sb],
        compiler_params=_cparams(("parallel", "parallel")),
    )(sc)


def _gelu2(x):
    return x * (1.0 + lax.erf(x * (2.0 ** -0.5)))


def _peer_kernel(h_ref, u_ref, vt_ref, c_ref, e1_ref, rk_ref, e2_ref, x1_ref, nf_ref, y_ref,
                 acc_ref, act_ref, p_ref):
    e = pl.program_id(1)
    ne = pl.num_programs(1)
    nrow = PEER_EXP_TILE // PEER_NKEYS

    @pl.when(e == 0)
    def _():
        acc_ref[...] = jnp.zeros_like(acc_ref)

    act_ref[...] = _dot_nt(u_ref[...], h_ref[...])
    base = pl.multiple_of(e * nrow, nrow)
    cuts = [c_ref[h, pl.ds(base, nrow), :] for h in range(PEER_HEADS)]
    e1s = [e1_ref[h, pl.ds(base, nrow), :] for h in range(PEER_HEADS)]
    for il in range(nrow):
        rows = slice(il * PEER_NKEYS, (il + 1) * PEER_NKEYS)
        w = None
        for h in range(PEER_HEADS):
            cut = cuts[h][il:il + 1, :].astype(BF16)
            e1 = e1s[h][il:il + 1, :].astype(BF16)
            term = jnp.where(rk_ref[h] < cut, e2_ref[h], jnp.zeros((), BF16)) * e1
            w = term if w is None else w + term
        p_ref[rows, :] = w * _gelu2(act_ref[rows, :].astype(BF16))
    acc_ref[...] += _dot(vt_ref[...], p_ref[...])

    @pl.when(e == ne - 1)
    def _():
        y = x1_ref[...] + acc_ref[...].T
        y_ref[...] = _rmsnorm(y, nf_ref[...])


def _peer(h2, u_bf16, vt_bf16, cut, e1, rank2, e2, x1, nf):
    T = h2.shape[0]
    tT = PEER_TOK_TILE
    eT = PEER_EXP_TILE
    sel = pl.BlockSpec((PEER_HEADS, PEER_NKEYS, tT), lambda t, e: (0, 0, t))
    return pl.pallas_call(
        _peer_kernel,
        name="peer",
        grid=(T // tT, PEER_EXPERTS // eT),
        in_specs=[pl.BlockSpec((tT, D_MODEL), lambda t, e: (t, 0)),
                  pl.BlockSpec((eT, D_MODEL), lambda t, e: (e, 0)),
                  pl.BlockSpec((D_MODEL, eT), lambda t, e: (0, e)),
                  sel, sel, sel, sel,
                  pl.BlockSpec((tT, D_MODEL), lambda t, e: (t, 0)),
                  pl.BlockSpec((1, D_MODEL), lambda t, e: (0, 0))],
        out_specs=pl.BlockSpec((tT, D_MODEL), lambda t, e: (t, 0)),
        out_shape=jax.ShapeDtypeStruct((T, D_MODEL), F32),
        scratch_shapes=[pltpu.VMEM((D_MODEL, tT), F32),
                        pltpu.VMEM((eT, tT), F32),
                        pltpu.VMEM((eT, tT), BF16)],
        compiler_params=_cparams(("parallel", "arbitrary")),
    )(h2, u_bf16, vt_bf16, cut, e1, rank2, e2, x1, nf)


def _rope_tables(S):
    half = HEAD_DIM // 2
    inv = ROPE_BASE ** (-jnp.arange(half, dtype=F32) / half)
    ang = jnp.arange(S, dtype=F32)[:, None] * inv[None, :]
    c = jnp.cos(ang)
    s = jnp.sin(ang)
    reps = LANES // HEAD_DIM
    cos_t = jnp.tile(jnp.concatenate([c, c], axis=1), (1, reps))
    sin_t = jnp.tile(jnp.concatenate([-s, s], axis=1), (1, reps))
    return cos_t, sin_t


def _trunk(x, w):
    B, S, _ = x.shape
    T = B * S
    x2 = x.reshape(T, D_MODEL)
    cos_t, sin_t = _rope_tables(S)
    p_rwkv, p_ret, p_gate = _inproj(x2, w["norm1_g"], w["w_in"], cos_t, sin_t)
    r, v, kap, lw, kd, b, bonus, g = _rwkv_prep(
        p_rwkv.reshape(B, S, RWKV_COLS), w["mu"], w["w0"], w["w2"], w["a0"], w["a2"], w["g2"],
        w["k_k"], w["k_a"], w["r_k"], w["bd"])
    orw = _rwkv_scan(r, v, kap, lw, kd, b)
    oret = _retention(p_ret.reshape(B, S, RET_COLS), w["lg"])
    x1, h2, sc = _merge(orw.reshape(2, T, WIDTH), bonus.reshape(T, WIDTH), g.reshape(T, WIDTH),
                        oret.reshape(2, T, WIDTH), p_ret, p_gate, x2,
                        w["ln_g"], w["ln_b"], w["ret_ln_g"], w["bd"], w["proj_a"], w["proj_b"], w["w_out"],
                        w["norm2_g"], w["wq"], w["sk"])
    cut, e1, rank2, e2 = _topk_prep(sc)
    y = _peer(h2, w["u"], w["vt"], cut, e1, rank2, e2, x1, w["normf_g"])
    return y.reshape(B, S, D_MODEL)


def kernel(x_prompt, x_sample, norm1_g, w_in, rwkv_mu, rwkv_w0, rwkv_w2, rwkv_a0, rwkv_a2, rwkv_g2, rwkv_k_k,
           rwkv_k_a, rwkv_r_k, rwkv_ln_g, rwkv_ln_b, ret_ln_g, proj_a, proj_b, w_out, norm2_g, peer_wq,
           peer_subkeys, peer_u, peer_v, normf_g):
    assert norm1_g.shape[0] == 1, "single-layer trunk"
    head_id = jnp.arange(WIDTH) // HEAD_DIM
    w = {
        "norm1_g": norm1_g[0][None, :],
        "w_in": w_in[0].astype(BF16),
        "mu": rwkv_mu[0],
        "w0": rwkv_w0[0],
        "w2": rwkv_w2[0].astype(BF16),
        "a0": rwkv_a0[0],
        "a2": rwkv_a2[0].astype(BF16),
        "g2": rwkv_g2[0].astype(BF16),
        "k_k": rwkv_k_k[0][None, :],
        "k_a": rwkv_k_a[0][None, :],
        "r_k": rwkv_r_k[0].reshape(1, WIDTH),
        "ln_g": rwkv_ln_g[0][None, :],
        "ln_b": rwkv_ln_b[0][None, :],
        "ret_ln_g": ret_ln_g[0][None, :],
        "bd": (head_id[:, None] == head_id[None, :]).astype(BF16),
        "lg": jnp.broadcast_to(jnp.log1p(-jnp.exp2(-5.0 - jnp.arange(N_HEADS, dtype=F32)))[:, None],
                               (N_HEADS, LANES)),
        "proj_a": proj_a[0].astype(BF16),
        "proj_b": proj_b[0].astype(BF16),
        "w_out": w_out[0].astype(BF16),
        "norm2_g": norm2_g[0][None, :],
        "wq": peer_wq[0].astype(BF16),
        "sk": peer_subkeys[0].reshape(2 * PEER_HEADS, PEER_NKEYS, 128).astype(BF16),
        "u": peer_u[0].astype(BF16),
        "vt": peer_v[0].T.astype(BF16),
        "normf_g": normf_g[None, :],
    }
    return (_trunk(x_prompt, w), _trunk(x_sample, w))
```
